```python
import jax, jax.numpy as jnp
from jax import lax
import numpy as np

D_MODEL = 1024
BATCH = 16
SEQ = 2048
DEPTH = 4

FOX_HEADS = 8
FOX_HEAD_DIM = 64
FOX_WIDTH = FOX_HEADS * FOX_HEAD_DIM
Q_BLOCK = 128
SGU_GROUPS = 8
SGU_WIDTH = 512
SGU_CHUNK = 128
CONV_WIDTH = 512
CONV_KERNEL = 31
POOL_WINDOWS = (2, 4, 8, 16)
POOL_WIDTH = 512
POOL_GROUP = POOL_WIDTH // len(POOL_WINDOWS)
N_BRANCH = 4
BRANCH_WIDTH = 512
IN_SIZES = (FOX_WIDTH, FOX_WIDTH, FOX_WIDTH, FOX_HEADS, SGU_WIDTH, SGU_WIDTH, 2 * CONV_WIDTH, POOL_WIDTH, N_BRANCH * D_MODEL)
IN_WIDTH = 3 * FOX_WIDTH + FOX_HEADS + 2 * SGU_WIDTH + 2 * CONV_WIDTH + POOL_WIDTH + N_BRANCH * D_MODEL
DENSE_FF = 2816
N_EXPERTS = 8
TOP_K = 2
EXPERT_FF = 3584
MOE_BLOCK = 256
N_DENSE = (DEPTH + 1) // 2
N_MOE = DEPTH // 2
EPS = 1e-6

kernel_name = "hybrid_gated_fox_sgu_conv_pool_moe"


def rms_norm(x, g):
    xf = x.astype(jnp.float32)
    y = xf * lax.rsqrt(jnp.mean(xf * xf, axis=-1, keepdims=True) + EPS)
    return (y * g.astype(jnp.float32)).astype(x.dtype)


def forget_attention(q, k, v, f_logit, b_f, g_q, g_k):
    B, S, _ = q.shape
    H, Dh = FOX_HEADS, FOX_HEAD_DIM
    nb = S // Q_BLOCK
    q = rms_norm(q.reshape(B, S, H, Dh), g_q)
    k = rms_norm(k.reshape(B, S, H, Dh), g_k).transpose(0, 2, 1, 3)
    v = v.reshape(B, S, H, Dh).transpose(0, 2, 1, 3)
    log_f = jax.nn.log_sigmoid((f_logit + b_f).astype(jnp.float32))
    c = jnp.cumsum(log_f, axis=1).transpose(0, 2, 1)
    qb = q.reshape(B, nb, Q_BLOCK, H, Dh).transpose(1, 0, 3, 2, 4)
    cqb = c.reshape(B, H, nb, Q_BLOCK).transpose(2, 0, 1, 3)
    kpos = jnp.arange(S)
    scale = Dh ** -0.5

    def block(args):
        qi, cqi, i = args
        s = jnp.einsum('bhqd,bhkd->bhqk', qi, k, preferred_element_type=jnp.float32) * scale
        s = s + cqi[..., None] - c[:, :, None, :]
        qpos = i * Q_BLOCK + jnp.arange(Q_BLOCK)
        s = jnp.where(kpos[None, :] <= qpos[:, None], s, -jnp.inf)
        p = jax.nn.softmax(s, axis=-1).astype(v.dtype)
        return jnp.einsum('bhqk,bhkd->bhqd', p, v)

    out = lax.map(block, (qb, cqb, jnp.arange(nb)))
    return out.transpose(1, 0, 3, 2, 4).reshape(B, S, H * Dh)


def chunked_sgu(u, v, g_v, w_s, b_s):
    B, S, C = u.shape
    L, G = SGU_CHUNK, SGU_GROUPS
    u = jax.nn.gelu(u)
    v = rms_norm(jax.nn.gelu(v), g_v)
    vc = v.reshape(B, S // L, L, G, C // G)
    w = jnp.where(jnp.tril(jnp.ones((L, L), dtype=bool)), w_s, jnp.zeros_like(w_s))
    mixed = jnp.einsum('gts,bnsgc->bntgc', w, vc) + b_s.T[None, None, :, :, None]
    return u * mixed.reshape(B, S, C)


def conformer_conv(a, w_dw, b_dw, g_c):
    val, gate = jnp.split(a, 2, axis=-1)
    y = val * jax.nn.sigmoid(gate)
    C = y.shape[-1]
    y = lax.conv_general_dilated(y, w_dw[:, None, :], window_strides=(1,), padding=[(CONV_KERNEL - 1, 0)],
                                 dimension_numbers=('NWC', 'WIO', 'NWC'), feature_group_count=C) + b_dw
    return jax.nn.silu(rms_norm(y, g_c))


def multiscale_pool(p, w_pg, ls):
    B, S, C = p.shape
    cs = jnp.pad(jnp.cumsum(p.astype(jnp.float32), axis=1), ((0, 0), (1, 0), (0, 0)))
    t = jnp.arange(S)
    outs = []
    for gi, w in enumerate(POOL_WINDOWS):
        csg = cs[..., gi * POOL_GROUP:(gi + 1) * POOL_GROUP]
        lo = jnp.maximum(t + 1 - w, 0)
        total = csg[:, 1:] - jnp.take(csg, lo, axis=1)
        cnt = jnp.minimum(t + 1, w).astype(jnp.float32)
        outs.append(total / cnt[None, :, None])
    pooled = jnp.concatenate(outs, axis=-1).astype(p.dtype) - p
    pooled = jnp.einsum('bsgc,gcd->bsgd', pooled.reshape(B, S, len(POOL_WINDOWS), POOL_GROUP), w_pg)
    return pooled.reshape(B, S, C) * ls


def dense_swiglu(h, w_g, w_u, w_d):
    return (jax.nn.silu(h @ w_g) * (h @ w_u)) @ w_d


def moe_swiglu(h, w_router, w_g, w_u, w_d):
    B, S, D = h.shape
    N = B * S
    NK = N * TOP_K
    xt = h.reshape(N, D)
    logits = (xt @ w_router).astype(jnp.float32)
    top_val, top_idx = lax.top_k(logits, TOP_K)
    gates = jax.nn.softmax(top_val, axis=-1)
    e_flat = top_idx.reshape(-1)
    tok_flat = jnp.arange(NK) // TOP_K
    g_flat = gates.reshape(-1)
    order = jnp.argsort(e_flat)
    e_sorted = e_flat[order]
    sizes = jnp.bincount(e_flat, length=N_EXPERTS)
    padded = (sizes + MOE_BLOCK - 1) // MOE_BLOCK * MOE_BLOCK
    start = jnp.cumsum(sizes) - sizes
    pend = jnp.cumsum(padded)
    pstart = pend - padded
    dest = pstart[e_sorted] + jnp.arange(NK) - start[e_sorted]
    P = -(-NK // MOE_BLOCK) * MOE_BLOCK + N_EXPERTS * MOE_BLOCK
    nb = P // MOE_BLOCK
    tok_buf = jnp.full((P,), N, dtype=jnp.int32).at[dest].set(tok_flat[order].astype(jnp.int32))
    g_buf = jnp.zeros((P,), jnp.float32).at[dest].set(g_flat[order])
    blk_e = jnp.minimum(jnp.searchsorted(pend, jnp.arange(nb) * MOE_BLOCK, side='right'), N_EXPERTS - 1)
    x_pad = jnp.concatenate([xt, jnp.zeros((1, D), xt.dtype)], axis=0)
    xs = x_pad[tok_buf].reshape(nb, MOE_BLOCK, D)

    def expert_block(args):
        xb, e = args
        return (jax.nn.silu(xb @ w_g[e]) * (xb @ w_u[e])) @ w_d[e]

    ys = lax.map(expert_block, (xs, blk_e)).reshape(P, D)
    ys = ys * g_buf[:, None].astype(ys.dtype)
    out = jax.ops.segment_sum(ys, tok_buf, num_segments=N + 1)[:N]
    return out.reshape(B, S, D)


def setup_inputs(seed: int = 0) -> dict:
    key = jax.random.key(seed)
    ks = jax.random.split(key, 26)
    nrm = lambda k, shape, s: jax.random.normal(k, shape, jnp.float32) * s
    L, D = DEPTH, D_MODEL
    return {
        "x": nrm(ks[0], (BATCH, SEQ, D), 1.0),
        "g_mix": 1.0 + nrm(ks[1], (L, D), 0.05),
        "w_in": nrm(ks[2], (L, D, IN_WIDTH), D ** -0.5),
        "b_forget": jax.random.uniform(ks[3], (L, FOX_HEADS), jnp.float32, 1.0, 6.0),
        "g_q": 1.0 + nrm(ks[4], (L, FOX_HEAD_DIM), 0.05),
        "g_k": 1.0 + nrm(ks[5], (L, FOX_HEAD_DIM), 0.05),
        "g_sgu": 1.0 + nrm(ks[6], (L, SGU_WIDTH), 0.05),
        "w_spatial": nrm(ks[7], (L, SGU_GROUPS, SGU_CHUNK, SGU_CHUNK), 0.5 * SGU_CHUNK ** -0.5),
        "b_spatial": 1.0 + nrm(ks[8], (L, SGU_GROUPS, SGU_CHUNK), 0.1),
        "w_dwconv": nrm(ks[9], (L, CONV_KERNEL, CONV_WIDTH), CONV_KERNEL ** -0.5),
        "b_dwconv": nrm(ks[10], (L, CONV_WIDTH), 0.02),
        "g_conv": 1.0 + nrm(ks[11], (L, CONV_WIDTH), 0.05),
        "w_pool": nrm(ks[12], (L, len(POOL_WINDOWS), POOL_GROUP, POOL_GROUP), POOL_GROUP ** -0.5),
        "pool_scale": 1.0 + nrm(ks[13], (L, POOL_WIDTH), 0.1),
        "w_branch": nrm(ks[14], (L, N_BRANCH, BRANCH_WIDTH, D), BRANCH_WIDTH ** -0.5),
        "b_gate": nrm(ks[15], (L, N_BRANCH, D), 0.02),
        "w_out": nrm(ks[16], (L, D, D), D ** -0.5),
        "g_ffn": 1.0 + nrm(ks[17], (L, D), 0.05),
        "w_ffn_gate": nrm(ks[18], (N_DENSE, D, DENSE_FF), D ** -0.5),
        "w_ffn_up": nrm(ks[19], (N_DENSE, D, DENSE_FF), D ** -0.5),
        "w_ffn_down": nrm(ks[20], (N_DENSE, DENSE_FF, D), DENSE_FF ** -0.5),
        "w_router": nrm(ks[21], (N_MOE, D, N_EXPERTS), D ** -0.5),
        "w_exp_gate": nrm(ks[22], (N_MOE, N_EXPERTS, D, EXPERT_FF), D ** -0.5),
        "w_exp_up": nrm(ks[23], (N_MOE, N_EXPERTS, D, EXPERT_FF), D ** -0.5),
        "w_exp_down": nrm(ks[24], (N_MOE, N_EXPERTS, EXPERT_FF, D), EXPERT_FF ** -0.5),
    }


def reference(x, g_mix, w_in, b_forget, g_q, g_k, g_sgu, w_spatial, b_spatial, w_dwconv, b_dwconv, g_conv,
              w_pool, pool_scale, w_branch, b_gate, w_out, g_ffn, w_ffn_gate, w_ffn_up, w_ffn_down,
              w_router, w_exp_gate, w_exp_up, w_exp_down):
    B, S, D = x.shape
    splits = np.cumsum(IN_SIZES)[:-1].tolist()
    for l in range(DEPTH):
        h = rms_norm(x, g_mix[l])
        proj = h @ w_in[l]
        q, k, v, f_logit, su, sv, conv_in, pool_in, gate_logits = jnp.split(proj, splits, axis=-1)
        y_fox = forget_attention(q, k, v, f_logit, b_forget[l], g_q[l], g_k[l])
        y_sgu = chunked_sgu(su, sv, g_sgu[l], w_spatial[l], b_spatial[l])
        y_conv = conformer_conv(conv_in, w_dwconv[l], b_dwconv[l], g_conv[l])
        y_pool = multiscale_pool(pool_in, w_pool[l], pool_scale[l])
        ys = jnp.stack([y_fox, y_sgu, y_conv, y_pool], axis=2)
        yb = jnp.einsum('bsnc,ncd->bsnd', ys, w_branch[l])
        gates = jax.nn.sigmoid(gate_logits.reshape(B, S, N_BRANCH, D) + b_gate[l])
        merged = jnp.einsum('bsnd,bsnd->bsd', gates, yb)
        x = x + merged @ w_out[l]
        h2 = rms_norm(x, g_ffn[l])
        if l % 2 == 0:
            i = l // 2
            x = x + dense_swiglu(h2, w_ffn_gate[i], w_ffn_up[i], w_ffn_down[i])
        else:
            i = l // 2
            x = x + moe_swiglu(h2, w_router[i], w_exp_gate[i], w_exp_up[i], w_exp_down[i])
    return x
```

```python
import functools
import math

import jax
import jax.numpy as jnp
import numpy as np
from jax import lax
from jax.experimental import pallas as pl
from jax.experimental.pallas import tpu as pltpu

F32 = jnp.float32
BF16 = jnp.bfloat16

EPS = 1e-6
LANES = 128
HEAD_DIM = 64
N_HEADS = 8
N_HEAD_PAIRS = N_HEADS // 2
SGU_CHUNK = 128
SGU_GROUPS = 8
CONV_KERNEL = 31
CONV_HALO = 32
POOL_WINDOWS = (2, 4, 8, 16)
N_BRANCH = 4
N_EXPERTS = 8
WIDTH = 512
NEG = -1e30
VMEM_LIMIT = 48 * 1024 * 1024


def _cparams(sem, vmem=VMEM_LIMIT):
    return pltpu.CompilerParams(dimension_semantics=sem, vmem_limit_bytes=vmem)


def _rms(x, g):
    return x * lax.rsqrt(jnp.mean(x * x, axis=-1, keepdims=True) + EPS) * g


def _inproj_body(x_ref, g_ref, w_ref, wf_ref, o_ref, f_ref, h_scr, *, tm, rc):
    @pl.when(pl.program_id(1) == 0)
    def _():
        def chunk(i, c):
            r = pl.multiple_of(i * rc, rc)
            h_scr[pl.ds(r, rc), :] = _rms(x_ref[pl.ds(r, rc), :], g_ref[...]).astype(BF16)
            return c
        lax.fori_loop(0, tm // rc, chunk, 0)
        f_ref[...] = jnp.dot(h_scr[...], wf_ref[...], preferred_element_type=F32)

    o_ref[...] = jnp.dot(h_scr[...], w_ref[...], preferred_element_type=F32).astype(BF16)


def _inproj(x, g, w, wf, *, tm, tn):
    n, d = x.shape
    ncol = w.shape[1]
    return pl.pallas_call(
        functools.partial(_inproj_body, tm=tm, rc=128),
        grid=(n // tm, ncol // tn),
        in_specs=[
            pl.BlockSpec((tm, d), lambda i, j: (i, 0)),
            pl.BlockSpec((1, d), lambda i, j: (0, 0)),
            pl.BlockSpec((d, tn), lambda i, j: (0, j)),
            pl.BlockSpec((d, LANES), lambda i, j: (0, 0)),
        ],
        out_specs=[
            pl.BlockSpec((tm, tn), lambda i, j: (i, j)),
            pl.BlockSpec((tm, LANES), lambda i, j: (i, 0)),
        ],
        out_shape=[jax.ShapeDtypeStruct((n, ncol), BF16), jax.ShapeDtypeStruct((n, LANES), F32)],
        scratch_shapes=[pltpu.VMEM((tm, d), BF16)],
        compiler_params=_cparams(("parallel", "arbitrary")),
        name="inproj",
    )(x, g, w, wf)


def _split3(x):
    hi = x.astype(BF16)
    r1 = x - hi.astype(F32)
    mid = r1.astype(BF16)
    lo = (r1 - mid.astype(F32)).astype(BF16)
    return hi, mid, lo


def _cumfg_body(f_ref, b_ref, cc_ref, cr_ref, *, s):
    blk = LANES
    r = lax.broadcasted_iota(jnp.int32, (blk, blk), 0)
    c = lax.broadcasted_iota(jnp.int32, (blk, blk), 1)
    tri = jnp.where(c <= r, 1.0, 0.0).astype(BF16)
    carry = jnp.zeros((1, LANES), F32)
    for j in range(s // blk):
        z = f_ref[j * blk:(j + 1) * blk, :] + b_ref[...]
        logf = jnp.minimum(z, 0.0) - jnp.log1p(jnp.exp(-jnp.abs(z)))
        hi, mid, lo = _split3(logf)
        cs = (jnp.dot(tri, hi, preferred_element_type=F32)
              + jnp.dot(tri, mid, preferred_element_type=F32)
              + jnp.dot(tri, lo, preferred_element_type=F32)) + carry
        cc_ref[j * blk:(j + 1) * blk, :] = cs
        cr_ref[0, :, j * blk:(j + 1) * blk] = cs.T[0:N_HEADS, :]
        carry = cs[blk - 1:blk, :]


def _cumfg(f, b, *, batch, s):
    n = f.shape[0]
    return pl.pallas_call(
        functools.partial(_cumfg_body, s=s),
        grid=(batch,),
        in_specs=[pl.BlockSpec((s, LANES), lambda i: (i, 0)), pl.BlockSpec((1, LANES), lambda i: (0, 0))],
        out_specs=[pl.BlockSpec((s, LANES), lambda i: (i, 0)), pl.BlockSpec((1, N_HEADS, s), lambda i: (i, 0, 0))],
        out_shape=[jax.ShapeDtypeStruct((n, LANES), F32), jax.ShapeDtypeStruct((batch, N_HEADS, s), F32)],
        compiler_params=_cparams(("parallel",)),
        name="cumfg",
    )(f, b)


def _attn_body(q_ref, k_ref, v_ref, cc_ref, cr_ref, gq_ref, gk_ref, o_ref, kn_scr, *, tq, s):
    hp = pl.program_id(1)
    qi = pl.program_id(2)
    lane = lax.broadcasted_iota(jnp.int32, (1, LANES), 1)
    lo = lane < HEAD_DIM

    def headnorm(x, g):
        x2 = x * x
        sa = jnp.sum(jnp.where(lo, x2, 0.0), axis=-1, keepdims=True)
        sb = jnp.sum(jnp.where(lo, 0.0, x2), axis=-1, keepdims=True)
        ms = jnp.where(lo, sa, sb) * (1.0 / HEAD_DIM)
        return x * lax.rsqrt(ms + EPS) * g

    @pl.when(qi == 0)
    def _():
        def chunk(i, c):
            r = pl.multiple_of(i * tq, tq)
            kn_scr[pl.ds(r, tq), :] = headnorm(k_ref[pl.ds(r, tq), :].astype(F32), gk_ref[...]).astype(BF16)
            return c
        lax.fori_loop(0, s // tq, chunk, 0)

    q = headnorm(q_ref[...].astype(F32), gq_ref[...]) * (HEAD_DIM ** -0.5)
    cc = cc_ref[...]
    qpos = qi * tq + lax.broadcasted_iota(jnp.int32, (tq, tq), 0)
    kloc = lax.broadcasted_iota(jnp.int32, (tq, tq), 1)

    outs = []
    for a in range(2):
        qh = (jnp.where(lo, q, 0.0) if a == 0 else jnp.where(lo, 0.0, q)).astype(BF16)
        cq = jnp.sum(jnp.where(lane == 2 * hp + a, cc, 0.0), axis=-1, keepdims=True)

        def kv_step(kb, carry, qh=qh, cq=cq, a=a):
            m, l, acc = carry
            ks = pl.multiple_of(kb * tq, tq)
            sc = lax.dot_general(qh, kn_scr[pl.ds(ks, tq), :], (((1,), (1,)), ((), ())),
                                 preferred_element_type=F32)
            ck = cr_ref[0, pl.ds(2 * hp + a, 1), pl.ds(ks, tq)]
            sc = sc + (cq - ck)
            sc = jnp.where(kloc + ks <= qpos, sc, NEG)
            m_new = jnp.maximum(m, jnp.max(sc, axis=-1, keepdims=True))
            alpha = jnp.exp(m - m_new)
            p = jnp.exp(sc - m_new)
            l = alpha * l + jnp.sum(p, axis=-1, keepdims=True)
            acc = alpha * acc + jnp.dot(p.astype(BF16), v_ref[pl.ds(ks, tq), :], preferred_element_type=F32)
            return m_new, l, acc

        init = (jnp.full((tq, 1), NEG, F32), jnp.zeros((tq, 1), F32), jnp.zeros((tq, LANES), F32))
        _, l, acc = lax.fori_loop(0, qi + 1, kv_step, init)
        outs.append(acc / l)
    o_ref[...] = jnp.where(lo, outs[0], outs[1]).astype(BF16)


def _attention(proj, cc, cr, gq2, gk2, *, batch, s, tq):
    n = proj.shape[0]
    nq = s // tq
    return pl.pallas_call(
        functools.partial(_attn_body, tq=tq, s=s),
        grid=(batch, N_HEAD_PAIRS, nq),
        in_specs=[
            pl.BlockSpec((tq, LANES), lambda b, h, i: (b * nq + i, h)),
            pl.BlockSpec((s, LANES), lambda b, h, i: (b, N_HEAD_PAIRS + h)),
            pl.BlockSpec((s, LANES), lambda b, h, i: (b, 2 * N_HEAD_PAIRS + h)),
            pl.BlockSpec((tq, LANES), lambda b, h, i: (b * nq + i, 0)),
            pl.BlockSpec((1, N_HEADS, s), lambda b, h, i: (b, 0, 0)),
            pl.BlockSpec((1, LANES), lambda b, h, i: (0, 0)),
            pl.BlockSpec((1, LANES), lambda b, h, i: (0, 0)),
        ],
        out_specs=pl.BlockSpec((tq, LANES), lambda b, h, i: (b * nq + i, h)),
        out_shape=jax.ShapeDtypeStruct((n, WIDTH), BF16),
        scratch_shapes=[pltpu.VMEM((s, LANES), BF16)],
        compiler_params=_cparams(("parallel", "parallel", "arbitrary")),
        name="fox_attention",
    )(proj, proj, proj, cc, cr, gq2, gk2)


def _branch_body(su_ref, sv_ref, cv_ref, cg_ref, p_ref, gs_ref, wsp_ref, bsp_ref, wdw_ref, bdw_ref, gc_ref,
                 wp_ref, ls_ref, ysgu_ref, yconv_ref, ypool_ref, ybuf, pbuf, *, t):
    ti = pl.program_id(1)
    lane = lax.broadcasted_iota(jnp.int32, (1, LANES), 1)
    lo = lane < (WIDTH // SGU_GROUPS)

    u = jax.nn.gelu(su_ref[...].astype(F32))
    vn = _rms(jax.nn.gelu(sv_ref[...].astype(F32)), gs_ref[...]).astype(BF16)
    r = lax.broadcasted_iota(jnp.int32, (SGU_CHUNK, SGU_CHUNK), 0)
    c = lax.broadcasted_iota(jnp.int32, (SGU_CHUNK, SGU_CHUNK), 1)
    wm = [jnp.where(c <= r, wsp_ref[g], 0.0).astype(BF16) for g in range(SGU_GROUPS)]
    for ch in range(t // SGU_CHUNK):
        rows = slice(ch * SGU_CHUNK, (ch + 1) * SGU_CHUNK)
        for gp in range(SGU_GROUPS // 2):
            cols = slice(gp * LANES, (gp + 1) * LANES)
            v2 = vn[rows, cols]
            m0 = jnp.dot(wm[2 * gp], v2, preferred_element_type=F32)
            m1 = jnp.dot(wm[2 * gp + 1], v2, preferred_element_type=F32)
            mixed = jnp.where(lo, m0, m1) + bsp_ref[:, cols]
            ysgu_ref[rows, cols] = (u[rows, cols] * mixed).astype(BF16)

    @pl.when(ti == 0)
    def _():
        ybuf[0:CONV_HALO, :] = jnp.zeros((CONV_HALO, WIDTH), F32)
        pbuf[0:t, :] = jnp.zeros((t, WIDTH), BF16)

    ybuf[CONV_HALO:CONV_HALO + t, :] = cv_ref[...].astype(F32) * jax.nn.sigmoid(cg_ref[...].astype(F32))
    rc = 32
    off = CONV_HALO - (CONV_KERNEL - 1)
    for i in range(t // rc):
        acc = jnp.zeros((rc, WIDTH), F32) + bdw_ref[...]
        for j in range(CONV_KERNEL):
            acc = acc + ybuf[i * rc + off + j:i * rc + off + j + rc, :] * wdw_ref[j:j + 1, :]
        yn = _rms(acc, gc_ref[...])
        yconv_ref[i * rc:(i + 1) * rc, :] = (yn * jax.nn.sigmoid(yn)).astype(BF16)
    ybuf[0:CONV_HALO, :] = ybuf[t:t + CONV_HALO, :]

    p = p_ref[...]
    pbuf[t:2 * t, :] = p
    rr = lax.broadcasted_iota(jnp.int32, (t, 2 * t), 0)
    kk = lax.broadcasted_iota(jnp.int32, (t, 2 * t), 1)
    d = rr + t - kk
    pos1 = ti * t + lax.broadcasted_iota(jnp.int32, (t, 1), 0) + 1
    for gi, w in enumerate(POOL_WINDOWS):
        cols = slice(gi * LANES, (gi + 1) * LANES)
        band = jnp.where((d >= 0) & (d < w), 1.0, 0.0).astype(BF16)
        total = jnp.dot(band, pbuf[:, cols], preferred_element_type=F32)
        cnt = jnp.minimum(pos1, w).astype(F32)
        z = (total / cnt - p[:, cols].astype(F32)).astype(BF16)
        ypool_ref[:, cols] = (jnp.dot(z, wp_ref[gi], preferred_element_type=F32) * ls_ref[:, cols]).astype(BF16)
    pbuf[0:t, :] = p


def _branches(proj, gs, wsp, bsp, wdw, bdw, gc, wp, ls, *, batch, s, t):
    n = proj.shape[0]
    nt = s // t
    seg = lambda k: pl.BlockSpec((t, WIDTH), lambda b, i, k=k: (b * nt + i, k))
    full = lambda shape: pl.BlockSpec(shape, lambda b, i: (0,) * len(shape))
    out = pl.BlockSpec((t, WIDTH), lambda b, i: (b * nt + i, 0))
    return pl.pallas_call(
        functools.partial(_branch_body, t=t),
        grid=(batch, nt),
        in_specs=[seg(3), seg(4), seg(5), seg(6), seg(7),
                  full((1, WIDTH)), full((SGU_GROUPS, SGU_CHUNK, SGU_CHUNK)), full((SGU_CHUNK, WIDTH)),
                  full((CONV_HALO, WIDTH)), full((1, WIDTH)), full((1, WIDTH)),
                  full((len(POOL_WINDOWS), LANES, LANES)), full((1, WIDTH))],
        out_specs=[out, out, out],
        out_shape=[jax.ShapeDtypeStruct((n, WIDTH), BF16)] * 3,
        scratch_shapes=[pltpu.VMEM((CONV_HALO + t, WIDTH), F32), pltpu.VMEM((2 * t, WIDTH), BF16)],
        compiler_params=_cparams(("parallel", "arbitrary")),
        name="branches",
    )(proj, proj, proj, proj, proj, gs, wsp, bsp, wdw, bdw, gc, wp, ls)


def _merge_body(yf_ref, ys_ref, yc_ref, yp_ref, gl_ref, bg_ref, x_ref, wb_ref, wo_ref, gf_ref, xo_ref, h_ref):
    d = x_ref.shape[1]
    merged = None
    for i, y_ref in enumerate((yf_ref, ys_ref, yc_ref, yp_ref)):
        yb = jnp.dot(y_ref[...], wb_ref[i], preferred_element_type=F32)
        gate = jax.nn.sigmoid(gl_ref[:, i * d:(i + 1) * d].astype(F32) + bg_ref[:, i * d:(i + 1) * d])
        merged = gate * yb if merged is None else merged + gate * yb
    xn = x_ref[...] + jnp.dot(merged.astype(BF16), wo_ref[...], preferred_element_type=F32)
    xo_ref[...] = xn
    h_ref[...] = _rms(xn, gf_ref[...]).astype(h_ref.dtype)


def _merge(yf, ys, yc, yp, proj, bg, x, wb, wo, gf, *, tm, h_dtype):
    n, d = x.shape
    row = lambda w: pl.BlockSpec((tm, w), lambda i: (i, 0))
    return pl.pallas_call(
        _merge_body,
        grid=(n // tm,),
        in_specs=[row(WIDTH), row(WIDTH), row(WIDTH), row(WIDTH),
                  pl.BlockSpec((tm, N_BRANCH * d), lambda i: (i, 1)),
                  pl.BlockSpec((1, N_BRANCH * d), lambda i: (0, 0)),
                  row(d),
                  pl.BlockSpec((N_BRANCH, WIDTH, d), lambda i: (0, 0, 0)),
                  pl.BlockSpec((d, d), lambda i: (0, 0)),
                  pl.BlockSpec((1, d), lambda i: (0, 0))],
        out_specs=[row(d), row(d)],
        out_shape=[jax.ShapeDtypeStruct((n, d), F32), jax.ShapeDtypeStruct((n, d), h_dtype)],
        compiler_params=_cparams(("parallel",)),
        name="merge",
    )(yf, ys, yc, yp, proj, bg, x, wb, wo, gf)


def _dense_ffn_body(h_ref, x_ref, wg_ref, wu_ref, wd_ref, o_ref, a_scr, *, tf):
    h = h_ref[...]
    ff = wg_ref.shape[1]
    for c in range(ff // tf):
        cols = slice(c * tf, (c + 1) * tf)
        g = jnp.dot(h, wg_ref[:, cols], preferred_element_type=F32)
        u = jnp.dot(h, wu_ref[:, cols], preferred_element_type=F32)
        a_scr[:, cols] = (g * jax.nn.sigmoid(g) * u).astype(BF16)
    o_ref[...] = x_ref[...] + jnp.dot(a_scr[...], wd_ref[...], preferred_element_type=F32)


def _dense_ffn(h, x, wg, wu, wd, *, tm, tf):
    n, d = x.shape
    ff = wg.shape[1]
    row = pl.BlockSpec((tm, d), lambda i: (i, 0))
    return pl.pallas_call(
        functools.partial(_dense_ffn_body, tf=tf),
        grid=(n // tm,),
        in_specs=[row, row,
                  pl.BlockSpec((d, ff), lambda i: (0, 0)),
                  pl.BlockSpec((d, ff), lambda i: (0, 0)),
                  pl.BlockSpec((ff, d), lambda i: (0, 0))],
        out_specs=row,
        out_shape=jax.ShapeDtypeStruct((n, d), F32),
        scratch_shapes=[pltpu.VMEM((tm, ff), BF16)],
        compiler_params=_cparams(("parallel",), vmem=56 * 1024 * 1024),
        name="dense_ffn",
    )(h, x, wg, wu, wd)


def _route_body(h_ref, wr_ref, info_ref, cnt_ref, carry, *, tm):
    i = pl.program_id(0)

    @pl.when(i == 0)
    def _():
        carry[...] = jnp.zeros_like(carry)

    lane = lax.broadcasted_iota(jnp.int32, (tm, LANES), 1)
    logits = jnp.dot(h_ref[...].astype(BF16), wr_ref[...], preferred_element_type=F32)
    logits = jnp.where(lane < N_EXPERTS, logits, NEG)
    lane_f = lane.astype(F32)
    m1 = jnp.max(logits, axis=-1, keepdims=True)
    i1 = jnp.min(jnp.where(logits == m1, lane_f, float(LANES)), axis=-1, keepdims=True)
    rest = jnp.where(lane_f == i1, NEG, logits)
    m2 = jnp.max(rest, axis=-1, keepdims=True)
    i2 = jnp.min(jnp.where(rest == m2, lane_f, float(LANES)), axis=-1, keepdims=True)
    e2 = jnp.exp(m2 - m1)
    g1 = 1.0 / (1.0 + e2)
    g2 = e2 / (1.0 + e2)

    hit1 = lane_f == i1
    hit2 = lane_f == i2
    onehot = jnp.where(hit1 | hit2, 1.0, 0.0).astype(BF16)
    r = lax.broadcasted_iota(jnp.int32, (tm, tm), 0)
    c = lax.broadcasted_iota(jnp.int32, (tm, tm), 1)
    before = jnp.where(c < r, 1.0, 0.0).astype(BF16)
    prior = jnp.dot(before, onehot, preferred_element_type=F32) + carry[0:1, :]
    rank1 = jnp.sum(jnp.where(hit1, prior, 0.0), axis=-1, keepdims=True)
    rank2 = jnp.sum(jnp.where(hit2, prior, 0.0), axis=-1, keepdims=True)
    total = carry[0:1, :] + jnp.sum(onehot.astype(F32), axis=0, keepdims=True)
    carry[0:1, :] = total
    cnt_ref[...] = jnp.broadcast_to(total, cnt_ref.shape)

    info = jnp.where(lane == 0, g1, 0.0)
    info = jnp.where(lane == 1, g2, info)
    info = jnp.where(lane == 2, i1, info)
    info = jnp.where(lane == 3, i2, info)
    info = jnp.where(lane == 4, rank1, info)
    info = jnp.where(lane == 5, rank2, info)
    info_ref[...] = info


def _route(h, wr, *, tm):
    n, d = h.shape
    return pl.pallas_call(
        functools.partial(_route_body, tm=tm),
        grid=(n // tm,),
        in_specs=[pl.BlockSpec((tm, d), lambda i: (i, 0)), pl.BlockSpec((d, LANES), lambda i: (0, 0))],
        out_specs=[pl.BlockSpec((tm, LANES), lambda i: (i, 0)), pl.BlockSpec((8, LANES), lambda i: (0, 0))],
        out_shape=[jax.ShapeDtypeStruct((n, LANES), F32), jax.ShapeDtypeStruct((8, LANES), F32)],
        scratch_shapes=[pltpu.VMEM((8, LANES), F32)],
        compiler_params=_cparams(("arbitrary",)),
        name="route",
    )(h, wr)


def _dispatch_body(d1_ref, d2_ref, h_ref, xs_in_ref, xs_ref, sem, *, td):
    del xs_in_ref
    base = pl.program_id(0) * td

    def copies(t):
        src = h_ref.at[pl.ds(t, 1), :]
        return (pltpu.make_async_copy(src, xs_ref.at[pl.ds(d1_ref[base + t], 1), :], sem),
                pltpu.make_async_copy(src, xs_ref.at[pl.ds(d2_ref[base + t], 1), :], sem))

    def start(t, c):
        for cp in copies(t):
            cp.start()
        return c

    def wait(t, c):
        for cp in copies(t):
            cp.wait()
        return c

    lax.fori_loop(0, td, start, 0)
    lax.fori_loop(0, td, wait, 0)


def _dispatch(d1, d2, h, xs_init, *, td):
    n, d = h.shape
    return pl.pallas_call(
        functools.partial(_dispatch_body, td=td),
        grid_spec=pltpu.PrefetchScalarGridSpec(
            num_scalar_prefetch=2,
            grid=(n // td,),
            in_specs=[pl.BlockSpec((td, d), lambda i, d1, d2: (i, 0)), pl.BlockSpec(memory_space=pl.ANY)],
            out_specs=pl.BlockSpec(memory_space=pl.ANY),
            scratch_shapes=[pltpu.SemaphoreType.DMA(())],
        ),
        out_shape=jax.ShapeDtypeStruct(xs_init.shape, xs_init.dtype),
        input_output_aliases={3: 0},
        compiler_params=_cparams(("arbitrary",)),
        name="dispatch",
    )(d1, d2, h, xs_init)


def _expert_body(be_ref, nu_ref, xs_ref, wg_ref, wu_ref, wd_ref, ys_ref, xb_scr, acc_scr):
    b = pl.program_id(0)
    f = pl.program_id(1)
    used = b < nu_ref[0]

    @pl.when(f == 0)
    def _():
        xb_scr[...] = xs_ref[...].astype(BF16)
        acc_scr[...] = jnp.zeros_like(acc_scr)

    @pl.when(used)
    def _():
        xb = xb_scr[...]
        g = jnp.dot(xb, wg_ref[...], preferred_element_type=F32)
        u = jnp.dot(xb, wu_ref[...], preferred_element_type=F32)
        a = (g * jax.nn.sigmoid(g) * u).astype(BF16)
        acc_scr[...] += jnp.dot(a, wd_ref[...], preferred_element_type=F32)

    @pl.when(f == pl.num_programs(1) - 1)
    def _():
        ys_ref[...] = acc_scr[...]


def _expert_ffn(blk_e, n_used, xs, wg, wu, wd, *, tmb, tf):
    p, d = xs.shape
    ff = wg.shape[2]
    return pl.pallas_call(
        _expert_body,
        grid_spec=pltpu.PrefetchScalarGridSpec(
            num_scalar_prefetch=2,
            grid=(p // tmb, ff // tf),
            in_specs=[
                pl.BlockSpec((tmb, d), lambda b, f, be, nu: (b, 0)),
                pl.BlockSpec((None, d, tf), lambda b, f, be, nu: (be[b], 0, f)),
                pl.BlockSpec((None, d, tf), lambda b, f, be, nu: (be[b], 0, f)),
                pl.BlockSpec((None, tf, d), lambda b, f, be, nu: (be[b], f, 0)),
            ],
            out_specs=pl.BlockSpec((tmb, d), lambda b, f, be, nu: (b, 0)),
            scratch_shapes=[pltpu.VMEM((tmb, d), BF16), pltpu.VMEM((tmb, d), F32)],
        ),
        out_shape=jax.ShapeDtypeStruct((p, d), F32),
        compiler_params=_cparams(("parallel", "arbitrary")),
        name="expert_ffn",
    )(blk_e, n_used, xs, wg, wu, wd)


def _combine_body(d1_ref, d2_ref, x_ref, info_ref, ys_ref, o_ref, ya, yb, sem, *, tc):
    base = pl.program_id(0) * tc

    def copies(t):
        return (pltpu.make_async_copy(ys_ref.at[pl.ds(d1_ref[base + t], 1), :], ya.at[pl.ds(t, 1), :], sem),
                pltpu.make_async_copy(ys_ref.at[pl.ds(d2_ref[base + t], 1), :], yb.at[pl.ds(t, 1), :], sem))

    def start(t, c):
        for cp in copies(t):
            cp.start()
        return c

    def wait(t, c):
        for cp in copies(t):
            cp.wait()
        return c

    lax.fori_loop(0, tc, start, 0)
    lax.fori_loop(0, tc, wait, 0)
    info = info_ref[...]
    o_ref[...] = x_ref[...] + info[:, 0:1] * ya[...] + info[:, 1:2] * yb[...]


def _combine(d1, d2, x, info, ys, *, tc):
    n, d = x.shape
    return pl.pallas_call(
        functools.partial(_combine_body, tc=tc),
        grid_spec=pltpu.PrefetchScalarGridSpec(
            num_scalar_prefetch=2,
            grid=(n // tc,),
            in_specs=[pl.BlockSpec((tc, d), lambda i, d1, d2: (i, 0)),
                      pl.BlockSpec((tc, LANES), lambda i, d1, d2: (i, 0)),
                      pl.BlockSpec(memory_space=pl.ANY)],
            out_specs=pl.BlockSpec((tc, d), lambda i, d1, d2: (i, 0)),
            scratch_shapes=[pltpu.VMEM((tc, d), F32), pltpu.VMEM((tc, d), F32), pltpu.SemaphoreType.DMA(())],
        ),
        out_shape=jax.ShapeDtypeStruct((n, d), F32),
        compiler_params=_cparams(("arbitrary",)),
        name="combine",
    )(d1, d2, x, info, ys)


def _tiles(n, s):
    return dict(
        tm_in=min(1024, n), tn_in=1024,
        tq=min(256, s),
        t_branch=min(256, s),
        tm_merge=min(512, n),
        tm_ffn=min(512, n), tf_ffn=256,
        tm_route=min(512, n),
        t_rows=min(256, n),
        tmb=512, tf_exp=512,
    )


def _moe_ffn(h, x, wr, wg, wu, wd, cfg):
    n, d = x.shape
    tmb = cfg["tmb"]
    info, counts = _route(h, wr, tm=cfg["tm_route"])
    sizes = counts[0, :N_EXPERTS].astype(jnp.int32)
    padded = (sizes + tmb - 1) // tmb * tmb
    pend = jnp.cumsum(padded)
    pstart = pend - padded
    e1 = info[:, 2].astype(jnp.int32)
    e2 = info[:, 3].astype(jnp.int32)
    d1 = pstart[e1] + info[:, 4].astype(jnp.int32)
    d2 = pstart[e2] + info[:, 5].astype(jnp.int32)
    p = (2 * n + tmb - 1) // tmb * tmb + N_EXPERTS * tmb
    nb = p // tmb
    blk_e = jnp.minimum(jnp.searchsorted(pend, jnp.arange(nb, dtype=jnp.int32) * tmb, side="right"),
                        N_EXPERTS - 1).astype(jnp.int32)
    n_used = (pend[-1:] // tmb).astype(jnp.int32)
    blk_e = jnp.where(jnp.arange(nb) < n_used[0], blk_e, blk_e[jnp.maximum(n_used[0] - 1, 0)])
    xs = _dispatch(d1, d2, h, jnp.zeros((p, d), F32), td=cfg["t_rows"])
    ys = _expert_ffn(blk_e, n_used, xs, wg, wu, wd, tmb=tmb, tf=cfg["tf_exp"])
    return _combine(d1, d2, x, info, ys, tc=cfg["t_rows"])


def kernel(x, g_mix, w_in, b_forget, g_q, g_k, g_sgu, w_spatial, b_spatial, w_dwconv, b_dwconv, g_conv, w_pool,
           pool_scale, w_branch, b_gate, w_out, g_ffn, w_ffn_gate, w_ffn_up, w_ffn_down, w_router, w_exp_gate,
           w_exp_up, w_exp_down):
    batch, s, d = x.shape
    n = batch * s
    depth = w_in.shape[0]
    cfg = _tiles(n, s)
    fox_w = N_HEADS * HEAD_DIM
    f_lo, f_hi = 3 * fox_w, 3 * fox_w + N_HEADS

    xf = x.reshape(n, d)
    for l in range(depth):
        w_main = jnp.concatenate([w_in[l][:, :f_lo], w_in[l][:, f_hi:]], axis=1).astype(BF16)
        w_f = jnp.pad(w_in[l][:, f_lo:f_hi], ((0, 0), (0, LANES - N_HEADS))).astype(BF16)
        proj, f_logit = _inproj(xf, g_mix[l][None], w_main, w_f, tm=cfg["tm_in"], tn=cfg["tn_in"])

        b_f = jnp.pad(b_forget[l], (0, LANES - N_HEADS))[None]
        cc, cr = _cumfg(f_logit, b_f, batch=batch, s=s)
        y_fox = _attention(proj, cc, cr, jnp.tile(g_q[l], 2)[None], jnp.tile(g_k[l], 2)[None],
                           batch=batch, s=s, tq=cfg["tq"])

        b_sp = jnp.repeat(b_spatial[l].T, WIDTH // SGU_GROUPS, axis=1)
        w_dw = jnp.pad(w_dwconv[l], ((0, CONV_HALO - CONV_KERNEL), (0, 0)))
        y_sgu, y_conv, y_pool = _branches(
            proj, g_sgu[l][None], w_spatial[l], b_sp, w_dw, b_dwconv[l][None], g_conv[l][None],
            w_pool[l].astype(BF16), pool_scale[l][None], batch=batch, s=s, t=cfg["t_branch"])

        moe = l % 2 == 1
        xf, h2 = _merge(y_fox, y_sgu, y_conv, y_pool, proj, b_gate[l].reshape(1, N_BRANCH * d), xf,
                        w_branch[l].astype(BF16), w_out[l].astype(BF16), g_ffn[l][None],
                        tm=cfg["tm_merge"], h_dtype=F32 if moe else BF16)
        i = l // 2
        if moe:
            w_r = jnp.pad(w_router[i], ((0, 0), (0, LANES - N_EXPERTS))).astype(BF16)
            xf = _moe_ffn(h2, xf, w_r, w_exp_gate[i].astype(BF16), w_exp_up[i].astype(BF16),
                          w_exp_down[i].astype(BF16), cfg)
        else:
            xf = _dense_ffn(h2, xf, w_ffn_gate[i].astype(BF16), w_ffn_up[i].astype(BF16),
                            w_ffn_down[i].astype(BF16), tm=cfg["tm_ffn"], tf=cfg["tf_ffn"])
    return xf.reshape(batch, s, d)
```

```python
import functools
import math

import jax
import jax.numpy as jnp
import numpy as np
from jax import lax
from jax.experimental import pallas as pl
from jax.experimental.pallas import tpu as pltpu

F32 = jnp.float32
BF16 = jnp.bfloat16

EPS = 1e-6
LANES = 128
SUBLANES = 8
HEAD_DIM = 64
N_HEADS = 8
N_HEAD_PAIRS = N_HEADS // 2
SGU_CHUNK = 128
SGU_GROUPS = 8
CONV_KERNEL = 31
CONV_HALO = 32
POOL_WINDOWS = (2, 4, 8, 16)
N_BRANCH = 4
N_EXPERTS = 8
WIDTH = 512
NEG = -1e30
LOG2E = math.log2(math.e)
VMEM_LIMIT = 48 * 1024 * 1024


def _cparams(sem, vmem=VMEM_LIMIT):
    return pltpu.CompilerParams(dimension_semantics=sem, vmem_limit_bytes=vmem)


def _rms(x, g):
    return x * lax.rsqrt(jnp.mean(x * x, axis=-1, keepdims=True) + EPS) * g


def _inproj_body(x_ref, g_ref, w_ref, wf_ref, o_ref, f_ref, h_scr, *, tm, rc):
    @pl.when(pl.program_id(1) == 0)
    def _():
        def chunk(i, c):
            r = pl.multiple_of(i * rc, rc)
            h_scr[pl.ds(r, rc), :] = _rms(x_ref[pl.ds(r, rc), :], g_ref[...]).astype(BF16)
            return c
        lax.fori_loop(0, tm // rc, chunk, 0)
        f_ref[...] = jnp.dot(h_scr[...], wf_ref[...], preferred_element_type=F32)

    o_ref[...] = jnp.dot(h_scr[...], w_ref[...], preferred_element_type=F32).astype(BF16)


def _inproj(x, g, w, wf, *, tm, tn):
    n, d = x.shape
    ncol = w.shape[1]
    return pl.pallas_call(
        functools.partial(_inproj_body, tm=tm, rc=128),
        grid=(n // tm, ncol // tn),
        in_specs=[
            pl.BlockSpec((tm, d), lambda i, j: (i, 0)),
            pl.BlockSpec((1, d), lambda i, j: (0, 0)),
            pl.BlockSpec((d, tn), lambda i, j: (0, j)),
            pl.BlockSpec((d, LANES), lambda i, j: (0, 0)),
        ],
        out_specs=[
            pl.BlockSpec((tm, tn), lambda i, j: (i, j)),
            pl.BlockSpec((tm, LANES), lambda i, j: (i, 0)),
        ],
        out_shape=[jax.ShapeDtypeStruct((n, ncol), BF16), jax.ShapeDtypeStruct((n, LANES), F32)],
        scratch_shapes=[pltpu.VMEM((tm, d), BF16)],
        compiler_params=_cparams(("parallel", "arbitrary")),
        name="inproj",
    )(x, g, w, wf)


def _split3(x):
    hi = x.astype(BF16)
    r1 = x - hi.astype(F32)
    mid = r1.astype(BF16)
    lo = (r1 - mid.astype(F32)).astype(BF16)
    return hi, mid, lo


def _cumfg_body(f_ref, b_ref, cc_ref, cr_ref, *, s):
    blk = LANES
    r = lax.broadcasted_iota(jnp.int32, (blk, blk), 0)
    c = lax.broadcasted_iota(jnp.int32, (blk, blk), 1)
    tri = jnp.where(c <= r, 1.0, 0.0).astype(BF16)
    carry = jnp.zeros((1, LANES), F32)
    for j in range(s // blk):
        z = f_ref[j * blk:(j + 1) * blk, :] + b_ref[...]
        logf = jnp.minimum(z, 0.0) - jnp.log1p(jnp.exp(-jnp.abs(z)))
        hi, mid, lo = _split3(logf)
        cs = (jnp.dot(tri, hi, preferred_element_type=F32)
              + jnp.dot(tri, mid, preferred_element_type=F32)
              + jnp.dot(tri, lo, preferred_element_type=F32)) + carry
        cc_ref[j * blk:(j + 1) * blk, :] = cs
        cr_ref[0, :, j * blk:(j + 1) * blk] = cs.T[0:N_HEADS, :]
        carry = cs[blk - 1:blk, :]


def _cumfg(f, b, *, batch, s):
    n = f.shape[0]
    return pl.pallas_call(
        functools.partial(_cumfg_body, s=s),
        grid=(batch,),
        in_specs=[pl.BlockSpec((s, LANES), lambda i: (i, 0)), pl.BlockSpec((1, LANES), lambda i: (0, 0))],
        out_specs=[pl.BlockSpec((s, LANES), lambda i: (i, 0)), pl.BlockSpec((1, N_HEADS, s), lambda i: (i, 0, 0))],
        out_shape=[jax.ShapeDtypeStruct((n, LANES), F32), jax.ShapeDtypeStruct((batch, N_HEADS, s), F32)],
        compiler_params=_cparams(("parallel",)),
        name="cumfg",
    )(f, b)


def _attn_body(q_ref, k_ref, v_ref, cc_ref, cr_ref, gq_ref, gk_ref, o_ref, qn_scr, kn_scr, s_scr, *, tq, s):
    hp = pl.program_id(1)
    lane = lax.broadcasted_iota(jnp.int32, (1, LANES), 1)
    lo = lane < HEAD_DIM

    def headnorm(x, g):
        x2 = x * x
        sa = jnp.sum(jnp.where(lo, x2, 0.0), axis=-1, keepdims=True)
        sb = jnp.sum(jnp.where(lo, 0.0, x2), axis=-1, keepdims=True)
        ms = jnp.where(lo, sa, sb) * (1.0 / HEAD_DIM)
        return x * lax.rsqrt(ms + EPS) * g

    def prep(i, c):
        r = pl.multiple_of(i * tq, tq)
        kn_scr[pl.ds(r, tq), :] = headnorm(k_ref[pl.ds(r, tq), :].astype(F32), gk_ref[...]).astype(BF16)
        q = headnorm(q_ref[pl.ds(r, tq), :].astype(F32), gq_ref[...]) * (HEAD_DIM ** -0.5 * LOG2E)
        qn_scr[0, pl.ds(r, tq), :] = jnp.where(lo, q, 0.0).astype(BF16)
        qn_scr[1, pl.ds(r, tq), :] = jnp.where(lo, 0.0, q).astype(BF16)
        return c
    lax.fori_loop(0, s // tq, prep, 0)

    causal = (lax.broadcasted_iota(jnp.int32, (tq, tq), 1) <= lax.broadcasted_iota(jnp.int32, (tq, tq), 0))
    for qi in range(s // tq):
        rows = slice(qi * tq, (qi + 1) * tq)
        cc = cc_ref[rows, :] * LOG2E
        outs = []
        for a in range(2):
            qh = qn_scr[a, rows, :]
            cq = jnp.sum(jnp.where(lane == 2 * hp + a, cc, 0.0), axis=-1, keepdims=True)
            mrun = jnp.full((tq, LANES), NEG, F32)
            for kb in range(qi + 1):
                kcols = slice(kb * tq, (kb + 1) * tq)
                sc = lax.dot_general(qh, kn_scr[kcols, :], (((1,), (1,)), ((), ())), preferred_element_type=F32)
                ck = cr_ref[0, pl.ds(2 * hp + a, 1), kcols] * LOG2E
                sc = (sc + cq) - ck
                if kb == qi:
                    sc = jnp.where(causal, sc, NEG)
                s_scr[a, :, kcols] = sc
                for h in range(tq // LANES):
                    mrun = jnp.maximum(mrun, sc[:, h * LANES:(h + 1) * LANES])
            mb = jnp.broadcast_to(jnp.max(mrun, axis=-1, keepdims=True), (tq, LANES))
            lrun = jnp.zeros((tq, LANES), F32)
            acc = jnp.zeros((tq, LANES), F32)
            for kb in range(qi + 1):
                kcols = slice(kb * tq, (kb + 1) * tq)
                ps = []
                for h in range(tq // LANES):
                    ph = jnp.exp2(s_scr[a, :, kb * tq + h * LANES:kb * tq + (h + 1) * LANES] - mb)
                    lrun = lrun + ph
                    ps.append(ph.astype(BF16))
                p = jnp.concatenate(ps, axis=1)
                acc = acc + jnp.dot(p, v_ref[kcols, :], preferred_element_type=F32)
            outs.append(acc / jnp.sum(lrun, axis=-1, keepdims=True))
        o_ref[rows, :] = jnp.where(lo, outs[0], outs[1]).astype(BF16)


def _attention(proj, cc, cr, gq2, gk2, *, batch, s, tq):
    n = proj.shape[0]
    return pl.pallas_call(
        functools.partial(_attn_body, tq=tq, s=s),
        grid=(batch, N_HEAD_PAIRS),
        in_specs=[
            pl.BlockSpec((s, LANES), lambda b, h: (b, h)),
            pl.BlockSpec((s, LANES), lambda b, h: (b, N_HEAD_PAIRS + h)),
            pl.BlockSpec((s, LANES), lambda b, h: (b, 2 * N_HEAD_PAIRS + h)),
            pl.BlockSpec((s, LANES), lambda b, h: (b, 0)),
            pl.BlockSpec((1, N_HEADS, s), lambda b, h: (b, 0, 0)),
            pl.BlockSpec((1, LANES), lambda b, h: (0, 0)),
            pl.BlockSpec((1, LANES), lambda b, h: (0, 0)),
        ],
        out_specs=pl.BlockSpec((s, LANES), lambda b, h: (b, h)),
        out_shape=jax.ShapeDtypeStruct((n, WIDTH), BF16),
        scratch_shapes=[pltpu.VMEM((2, s, LANES), BF16), pltpu.VMEM((s, LANES), BF16),
                        pltpu.VMEM((2, tq, s), F32)],
        compiler_params=_cparams(("parallel", "parallel")),
        name="fox_attention",
    )(proj, proj, proj, cc, cr, gq2, gk2)


def _branch_body(su_ref, sv_ref, cv_ref, cg_ref, p_ref, gs_ref, wsp_ref, bsp_ref, wdw_ref, bdw_ref, gc_ref,
                 wp_ref, ls_ref, ysgu_ref, yconv_ref, ypool_ref, ybuf, pbuf, cacc, *, t):
    ti = pl.program_id(1)
    lane = lax.broadcasted_iota(jnp.int32, (1, LANES), 1)
    lo = lane < (WIDTH // SGU_GROUPS)

    u = jax.nn.gelu(su_ref[...].astype(F32))
    vn = _rms(jax.nn.gelu(sv_ref[...].astype(F32)), gs_ref[...]).astype(BF16)
    r = lax.broadcasted_iota(jnp.int32, (SGU_CHUNK, SGU_CHUNK), 0)
    c = lax.broadcasted_iota(jnp.int32, (SGU_CHUNK, SGU_CHUNK), 1)
    wm = [jnp.where(c <= r, wsp_ref[g], 0.0).astype(BF16) for g in range(SGU_GROUPS)]
    for ch in range(t // SGU_CHUNK):
        rows = slice(ch * SGU_CHUNK, (ch + 1) * SGU_CHUNK)
        for gp in range(SGU_GROUPS // 2):
            cols = slice(gp * LANES, (gp + 1) * LANES)
            v2 = vn[rows, cols]
            m0 = jnp.dot(wm[2 * gp], v2, preferred_element_type=F32)
            m1 = jnp.dot(wm[2 * gp + 1], v2, preferred_element_type=F32)
            mixed = jnp.where(lo, m0, m1) + bsp_ref[:, cols]
            ysgu_ref[rows, cols] = (u[rows, cols] * mixed).astype(BF16)

    @pl.when(ti == 0)
    def _():
        ybuf[0:CONV_HALO, :] = jnp.zeros((CONV_HALO, WIDTH), F32)
        pbuf[0:t, :] = jnp.zeros((t, WIDTH), BF16)
        ybuf[CONV_HALO + t:CONV_HALO + t + SUBLANES, :] = jnp.zeros((SUBLANES, WIDTH), F32)

    ybuf[CONV_HALO:CONV_HALO + t, :] = cv_ref[...].astype(F32) * jax.nn.sigmoid(cg_ref[...].astype(F32))
    rc = 64
    off = CONV_HALO - (CONV_KERNEL - 1)
    for i in range(t // rc):
        for lg in range(WIDTH // LANES):
            cols = slice(lg * LANES, (lg + 1) * LANES)
            acc = jnp.zeros((rc, LANES), F32) + bdw_ref[:, cols]
            for r in range(SUBLANES):
                part = None
                for a in range((off + CONV_KERNEL - 1) // SUBLANES + 1):
                    o = SUBLANES * a + r - off
                    if 0 <= o < CONV_KERNEL:
                        base = i * rc + SUBLANES * a
                        term = ybuf[base:base + rc + SUBLANES, cols] * wdw_ref[o:o + 1, cols]
                        part = term if part is None else part + term
                acc = acc + part[r:r + rc, :]
            cacc[i * rc:(i + 1) * rc, cols] = acc
    nc = 32
    for i in range(t // nc):
        yn = _rms(cacc[i * nc:(i + 1) * nc, :], gc_ref[...])
        yconv_ref[i * nc:(i + 1) * nc, :] = (yn * jax.nn.sigmoid(yn)).astype(BF16)
    ybuf[0:CONV_HALO, :] = ybuf[t:t + CONV_HALO, :]

    p = p_ref[...]
    pbuf[t:2 * t, :] = p
    rr = lax.broadcasted_iota(jnp.int32, (t, 2 * t), 0)
    kk = lax.broadcasted_iota(jnp.int32, (t, 2 * t), 1)
    d = rr + t - kk
    pos1 = ti * t + lax.broadcasted_iota(jnp.int32, (t, 1), 0) + 1
    for gi, w in enumerate(POOL_WINDOWS):
        cols = slice(gi * LANES, (gi + 1) * LANES)
        band = jnp.where((d >= 0) & (d < w), 1.0, 0.0).astype(BF16)
        total = jnp.dot(band, pbuf[:, cols], preferred_element_type=F32)
        cnt = jnp.minimum(pos1, w).astype(F32)
        z = (total / cnt - p[:, cols].astype(F32)).astype(BF16)
        ypool_ref[:, cols] = (jnp.dot(z, wp_ref[gi], preferred_element_type=F32) * ls_ref[:, cols]).astype(BF16)
    pbuf[0:t, :] = p


def _branches(proj, gs, wsp, bsp, wdw, bdw, gc, wp, ls, *, batch, s, t):
    n = proj.shape[0]
    nt = s // t
    seg = lambda k: pl.BlockSpec((t, WIDTH), lambda b, i, k=k: (b * nt + i, k))
    full = lambda shape: pl.BlockSpec(shape, lambda b, i: (0,) * len(shape))
    out = pl.BlockSpec((t, WIDTH), lambda b, i: (b * nt + i, 0))
    return pl.pallas_call(
        functools.partial(_branch_body, t=t),
        grid=(batch, nt),
        in_specs=[seg(3), seg(4), seg(5), seg(6), seg(7),
                  full((1, WIDTH)), full((SGU_GROUPS, SGU_CHUNK, SGU_CHUNK)), full((SGU_CHUNK, WIDTH)),
                  full((CONV_HALO, WIDTH)), full((1, WIDTH)), full((1, WIDTH)),
                  full((len(POOL_WINDOWS), LANES, LANES)), full((1, WIDTH))],
        out_specs=[out, out, out],
        out_shape=[jax.ShapeDtypeStruct((n, WIDTH), BF16)] * 3,
        scratch_shapes=[pltpu.VMEM((CONV_HALO + t + SUBLANES, WIDTH), F32), pltpu.VMEM((2 * t, WIDTH), BF16),
                        pltpu.VMEM((t, WIDTH), F32)],
        compiler_params=_cparams(("parallel", "arbitrary")),
        name="branches",
    )(proj, proj, proj, proj, proj, gs, wsp, bsp, wdw, bdw, gc, wp, ls)


def _merge_body(yf_ref, ys_ref, yc_ref, yp_ref, gl_ref, bg_ref, x_ref, wb_ref, wo_ref, gf_ref, xo_ref, h_ref):
    d = x_ref.shape[1]
    merged = None
    for i, y_ref in enumerate((yf_ref, ys_ref, yc_ref, yp_ref)):
        yb = jnp.dot(y_ref[...], wb_ref[i], preferred_element_type=F32)
        gate = jax.nn.sigmoid(gl_ref[:, i * d:(i + 1) * d].astype(F32) + bg_ref[:, i * d:(i + 1) * d])
        merged = gate * yb if merged is None else merged + gate * yb
    xn = x_ref[...] + jnp.dot(merged.astype(BF16), wo_ref[...], preferred_element_type=F32)
    xo_ref[...] = xn
    h_ref[...] = _rms(xn, gf_ref[...]).astype(h_ref.dtype)


def _merge(yf, ys, yc, yp, proj, bg, x, wb, wo, gf, *, tm, h_dtype):
    n, d = x.shape
    row = lambda w: pl.BlockSpec((tm, w), lambda i: (i, 0))
    return pl.pallas_call(
        _merge_body,
        grid=(n // tm,),
        in_specs=[row(WIDTH), row(WIDTH), row(WIDTH), row(WIDTH),
                  pl.BlockSpec((tm, N_BRANCH * d), lambda i: (i, 1)),
                  pl.BlockSpec((1, N_BRANCH * d), lambda i: (0, 0)),
                  row(d),
                  pl.BlockSpec((N_BRANCH, WIDTH, d), lambda i: (0, 0, 0)),
                  pl.BlockSpec((d, d), lambda i: (0, 0)),
                  pl.BlockSpec((1, d), lambda i: (0, 0))],
        out_specs=[row(d), row(d)],
        out_shape=[jax.ShapeDtypeStruct((n, d), F32), jax.ShapeDtypeStruct((n, d), h_dtype)],
        compiler_params=_cparams(("parallel",)),
        name="merge",
    )(yf, ys, yc, yp, proj, bg, x, wb, wo, gf)


def _dense_ffn_body(h_ref, x_ref, wg_ref, wu_ref, wd_ref, o_ref, a_scr, *, tf):
    h = h_ref[...]
    ff = wg_ref.shape[1]
    for c in range(ff // tf):
        cols = slice(c * tf, (c + 1) * tf)
        g = jnp.dot(h, wg_ref[:, cols], preferred_element_type=F32)
        u = jnp.dot(h, wu_ref[:, cols], preferred_element_type=F32)
        a_scr[:, cols] = (g * jax.nn.sigmoid(g) * u).astype(BF16)
    o_ref[...] = x_ref[...] + jnp.dot(a_scr[...], wd_ref[...], preferred_element_type=F32)


def _dense_ffn(h, x, wg, wu, wd, *, tm, tf):
    n, d = x.shape
    ff = wg.shape[1]
    row = pl.BlockSpec((tm, d), lambda i: (i, 0))
    return pl.pallas_call(
        functools.partial(_dense_ffn_body, tf=tf),
        grid=(n // tm,),
        in_specs=[row, row,
                  pl.BlockSpec((d, ff), lambda i: (0, 0)),
                  pl.BlockSpec((d, ff), lambda i: (0, 0)),
                  pl.BlockSpec((ff, d), lambda i: (0, 0))],
        out_specs=row,
        out_shape=jax.ShapeDtypeStruct((n, d), F32),
        scratch_shapes=[pltpu.VMEM((tm, ff), BF16)],
        compiler_params=_cparams(("parallel",), vmem=56 * 1024 * 1024),
        name="dense_ffn",
    )(h, x, wg, wu, wd)


def _route_body(h_ref, wr_ref, info_ref, cnt_ref, carry, *, tm):
    i = pl.program_id(0)

    @pl.when(i == 0)
    def _():
        carry[...] = jnp.zeros_like(carry)

    lane = lax.broadcasted_iota(jnp.int32, (tm, LANES), 1)
    logits = jnp.dot(h_ref[...].astype(BF16), wr_ref[...], preferred_element_type=F32)
    logits = jnp.where(lane < N_EXPERTS, logits, NEG)
    lane_f = lane.astype(F32)
    m1 = jnp.max(logits, axis=-1, keepdims=True)
    i1 = jnp.min(jnp.where(logits == m1, lane_f, float(LANES)), axis=-1, keepdims=True)
    rest = jnp.where(lane_f == i1, NEG, logits)
    m2 = jnp.max(rest, axis=-1, keepdims=True)
    i2 = jnp.min(jnp.where(rest == m2, lane_f, float(LANES)), axis=-1, keepdims=True)
    e2 = jnp.exp(m2 - m1)
    g1 = 1.0 / (1.0 + e2)
    g2 = e2 / (1.0 + e2)

    hit1 = lane_f == i1
    hit2 = lane_f == i2
    onehot = jnp.where(hit1 | hit2, 1.0, 0.0).astype(BF16)
    r = lax.broadcasted_iota(jnp.int32, (tm, tm), 0)
    c = lax.broadcasted_iota(jnp.int32, (tm, tm), 1)
    before = jnp.where(c < r, 1.0, 0.0).astype(BF16)
    prior = jnp.dot(before, onehot, preferred_element_type=F32) + carry[0:1, :]
    rank1 = jnp.sum(jnp.where(hit1, prior, 0.0), axis=-1, keepdims=True)
    rank2 = jnp.sum(jnp.where(hit2, prior, 0.0), axis=-1, keepdims=True)
    total = carry[0:1, :] + jnp.sum(onehot.astype(F32), axis=0, keepdims=True)
    carry[0:1, :] = total
    cnt_ref[...] = jnp.broadcast_to(total, cnt_ref.shape)

    info = jnp.where(lane == 0, g1, 0.0)
    info = jnp.where(lane == 1, g2, info)
    info = jnp.where(lane == 2, i1, info)
    info = jnp.where(lane == 3, i2, info)
    info = jnp.where(lane == 4, rank1, info)
    info = jnp.where(lane == 5, rank2, info)
    info_ref[...] = info


def _route(h, wr, *, tm):
    n, d = h.shape
    return pl.pallas_call(
        functools.partial(_route_body, tm=tm),
        grid=(n // tm,),
        in_specs=[pl.BlockSpec((tm, d), lambda i: (i, 0)), pl.BlockSpec((d, LANES), lambda i: (0, 0))],
        out_specs=[pl.BlockSpec((tm, LANES), lambda i: (i, 0)), pl.BlockSpec((8, LANES), lambda i: (0, 0))],
        out_shape=[jax.ShapeDtypeStruct((n, LANES), F32), jax.ShapeDtypeStruct((8, LANES), F32)],
        scratch_shapes=[pltpu.VMEM((8, LANES), F32)],
        compiler_params=_cparams(("arbitrary",)),
        name="route",
    )(h, wr)


def _dispatch_body(d1_ref, d2_ref, h_ref, xs_in_ref, xs_ref, sem, *, td):
    del xs_in_ref
    base = pl.program_id(0) * td

    def copies(t):
        src = h_ref.at[pl.ds(t, 1), :]
        return (pltpu.make_async_copy(src, xs_ref.at[pl.ds(d1_ref[base + t], 1), :], sem),
                pltpu.make_async_copy(src, xs_ref.at[pl.ds(d2_ref[base + t], 1), :], sem))

    def start(t, c):
        for cp in copies(t):
            cp.start()
        return c

    def wait(t, c):
        for cp in copies(t):
            cp.wait()
        return c

    lax.fori_loop(0, td, start, 0)
    lax.fori_loop(0, td, wait, 0)


def _dispatch(d1, d2, h, xs_init, *, td):
    n, d = h.shape
    return pl.pallas_call(
        functools.partial(_dispatch_body, td=td),
        grid_spec=pltpu.PrefetchScalarGridSpec(
            num_scalar_prefetch=2,
            grid=(n // td,),
            in_specs=[pl.BlockSpec((td, d), lambda i, d1, d2: (i, 0)), pl.BlockSpec(memory_space=pl.ANY)],
            out_specs=pl.BlockSpec(memory_space=pl.ANY),
            scratch_shapes=[pltpu.SemaphoreType.DMA(())],
        ),
        out_shape=jax.ShapeDtypeStruct(xs_init.shape, xs_init.dtype),
        input_output_aliases={3: 0},
        compiler_params=_cparams(("arbitrary",)),
        name="dispatch",
    )(d1, d2, h, xs_init)


def _expert_body(be_ref, nu_ref, xs_ref, wg_ref, wu_ref, wd_ref, ys_ref, xb_scr, a_scr, *, tc):
    del be_ref
    b = pl.program_id(0)
    f = pl.program_id(1)
    used = b < nu_ref[0]

    @pl.when(f == 0)
    def _():
        xb_scr[...] = xs_ref[...].astype(BF16)

    @pl.when(jnp.logical_not(used))
    def _():
        ys_ref[...] = jnp.zeros_like(ys_ref)

    @pl.when(used)
    def _():
        xb = xb_scr[...]
        for c in range(wg_ref.shape[1] // tc):
            cols = slice(c * tc, (c + 1) * tc)
            g = jnp.dot(xb, wg_ref[:, cols], preferred_element_type=F32)
            u = jnp.dot(xb, wu_ref[:, cols], preferred_element_type=F32)
            a_scr[:, cols] = (g * jax.nn.sigmoid(g) * u).astype(BF16)
        y = jnp.dot(a_scr[...], wd_ref[...], preferred_element_type=F32)

        @pl.when(f == 0)
        def _():
            ys_ref[...] = y

        @pl.when(f > 0)
        def _():
            ys_ref[...] += y


def _expert_ffn(blk_e, n_used, xs, wg, wu, wd, *, tmb, tf):
    p, d = xs.shape
    ff = wg.shape[2]
    return pl.pallas_call(
        functools.partial(_expert_body, tc=256),
        grid_spec=pltpu.PrefetchScalarGridSpec(
            num_scalar_prefetch=2,
            grid=(p // tmb, ff // tf),
            in_specs=[
                pl.BlockSpec((tmb, d), lambda b, f, be, nu: (b, 0)),
                pl.BlockSpec((None, d, tf), lambda b, f, be, nu: (be[b], 0, f)),
                pl.BlockSpec((None, d, tf), lambda b, f, be, nu: (be[b], 0, f)),
                pl.BlockSpec((None, tf, d), lambda b, f, be, nu: (be[b], f, 0)),
            ],
            out_specs=pl.BlockSpec((tmb, d), lambda b, f, be, nu: (b, 0)),
            scratch_shapes=[pltpu.VMEM((tmb, d), BF16), pltpu.VMEM((tmb, tf), BF16)],
        ),
        out_shape=jax.ShapeDtypeStruct((p, d), F32),
        compiler_params=_cparams(("parallel", "arbitrary")),
        name="expert_ffn",
    )(blk_e, n_used, xs, wg, wu, wd)


def _combine_body(d1_ref, d2_ref, x_ref, info_ref, ys_ref, o_ref, ya, yb, sem, *, tc):
    base = pl.program_id(0) * tc

    def copies(t):
        return (pltpu.make_async_copy(ys_ref.at[pl.ds(d1_ref[base + t], 1), :], ya.at[pl.ds(t, 1), :], sem),
                pltpu.make_async_copy(ys_ref.at[pl.ds(d2_ref[base + t], 1), :], yb.at[pl.ds(t, 1), :], sem))

    def start(t, c):
        for cp in copies(t):
            cp.start()
        return c

    def wait(t, c):
        for cp in copies(t):
            cp.wait()
        return c

    lax.fori_loop(0, tc, start, 0)
    lax.fori_loop(0, tc, wait, 0)
    info = info_ref[...]
    o_ref[...] = x_ref[...] + info[:, 0:1] * ya[...] + info[:, 1:2] * yb[...]


def _combine(d1, d2, x, info, ys, *, tc):
    n, d = x.shape
    return pl.pallas_call(
        functools.partial(_combine_body, tc=tc),
        grid_spec=pltpu.PrefetchScalarGridSpec(
            num_scalar_prefetch=2,
            grid=(n // tc,),
            in_specs=[pl.BlockSpec((tc, d), lambda i, d1, d2: (i, 0)),
                      pl.BlockSpec((tc, LANES), lambda i, d1, d2: (i, 0)),
                      pl.BlockSpec(memory_space=pl.ANY)],
            out_specs=pl.BlockSpec((tc, d), lambda i, d1, d2: (i, 0)),
            scratch_shapes=[pltpu.VMEM((tc, d), F32), pltpu.VMEM((tc, d), F32), pltpu.SemaphoreType.DMA(())],
        ),
        out_shape=jax.ShapeDtypeStruct((n, d), F32),
        compiler_params=_cparams(("arbitrary",)),
        name="combine",
    )(d1, d2, x, info, ys)


def _tiles(n, s):
    return dict(
        tm_in=min(1024, n), tn_in=1024,
        tq=min(256, s),
        t_branch=min(256, s),
        tm_merge=min(512, n),
        tm_ffn=min(512, n), tf_ffn=256,
        tm_route=min(512, n),
        t_rows=min(256, n),
        tmb=512, tf_exp=1792,
    )


def _moe_ffn(h, x, wr, wg, wu, wd, cfg):
    n, d = x.shape
    tmb = cfg["tmb"]
    info, counts = _route(h, wr, tm=cfg["tm_route"])
    sizes = counts[0, :N_EXPERTS].astype(jnp.int32)
    padded = (sizes + tmb - 1) // tmb * tmb
    pend = jnp.cumsum(padded)
    pstart = pend - padded
    e1 = info[:, 2].astype(jnp.int32)
    e2 = info[:, 3].astype(jnp.int32)
    d1 = pstart[e1] + info[:, 4].astype(jnp.int32)
    d2 = pstart[e2] + info[:, 5].astype(jnp.int32)
    p = (2 * n + tmb - 1) // tmb * tmb + N_EXPERTS * tmb
    nb = p // tmb
    blk_e = jnp.minimum(jnp.searchsorted(pend, jnp.arange(nb, dtype=jnp.int32) * tmb, side="right"),
                        N_EXPERTS - 1).astype(jnp.int32)
    n_used = (pend[-1:] // tmb).astype(jnp.int32)
    blk_e = jnp.where(jnp.arange(nb) < n_used[0], blk_e, blk_e[jnp.maximum(n_used[0] - 1, 0)])
    xs = _dispatch(d1, d2, h, jnp.zeros((p, d), F32), td=cfg["t_rows"])
    ys = _expert_ffn(blk_e, n_used, xs, wg, wu, wd, tmb=tmb, tf=cfg["tf_exp"])
    return _combine(d1, d2, x, info, ys, tc=cfg["t_rows"])


def kernel(x, g_mix, w_in, b_forget, g_q, g_k, g_sgu, w_spatial, b_spatial, w_dwconv, b_dwconv, g_conv, w_pool,
           pool_scale, w_branch, b_gate, w_out, g_ffn, w_ffn_gate, w_ffn_up, w_ffn_down, w_router, w_exp_gate,
           w_exp_up, w_exp_down):
    batch, s, d = x.shape
    n = batch * s
    depth = w_in.shape[0]
    cfg = _tiles(n, s)
    fox_w = N_HEADS * HEAD_DIM
    f_lo, f_hi = 3 * fox_w, 3 * fox_w + N_HEADS

    xf = x.reshape(n, d)
    for l in range(depth):
        w_main = jnp.concatenate([w_in[l][:, :f_lo], w_in[l][:, f_hi:]], axis=1).astype(BF16)
        w_f = jnp.pad(w_in[l][:, f_lo:f_hi], ((0, 0), (0, LANES - N_HEADS))).astype(BF16)
        proj, f_logit = _inproj(xf, g_mix[l][None], w_main, w_f, tm=cfg["tm_in"], tn=cfg["tn_in"])

        b_f = jnp.pad(b_forget[l], (0, LANES - N_HEADS))[None]
        cc, cr = _cumfg(f_logit, b_f, batch=batch, s=s)
        y_fox = _attention(proj, cc, cr, jnp.tile(g_q[l], 2)[None], jnp.tile(g_k[l], 2)[None],
                           batch=batch, s=s, tq=cfg["tq"])

        b_sp = jnp.repeat(b_spatial[l].T, WIDTH // SGU_GROUPS, axis=1)
        w_dw = jnp.pad(w_dwconv[l], ((0, CONV_HALO - CONV_KERNEL), (0, 0)))
        y_sgu, y_conv, y_pool = _branches(
            proj, g_sgu[l][None], w_spatial[l], b_sp, w_dw, b_dwconv[l][None], g_conv[l][None],
            w_pool[l].astype(BF16), pool_scale[l][None], batch=batch, s=s, t=cfg["t_branch"])

        moe = l % 2 == 1
        xf, h2 = _merge(y_fox, y_sgu, y_conv, y_pool, proj, b_gate[l].reshape(1, N_BRANCH * d), xf,
                        w_branch[l].astype(BF16), w_out[l].astype(BF16), g_ffn[l][None],
                        tm=cfg["tm_merge"], h_dtype=F32 if moe else BF16)
        i = l // 2
        if moe:
            w_r = jnp.pad(w_router[i], ((0, 0), (0, LANES - N_EXPERTS))).astype(BF16)
            xf = _moe_ffn(h2, xf, w_r, w_exp_gate[i].astype(BF16), w_exp_up[i].astype(BF16),
                          w_exp_down[i].astype(BF16), cfg)
        else:
            xf = _dense_ffn(h2, xf, w_ffn_gate[i].astype(BF16), w_ffn_up[i].astype(BF16),
                            w_ffn_down[i].astype(BF16), tm=cfg["tm_ffn"], tf=cfg["tf_ffn"])
    return xf.reshape(batch, s, d)
```

```python
import functools
import math

import jax
import jax.numpy as jnp
import numpy as np
from jax import lax
from jax.experimental import pallas as pl
from jax.experimental.pallas import tpu as pltpu

F32 = jnp.float32
BF16 = jnp.bfloat16

EPS = 1e-6
LANES = 128
SUBLANES = 8
HEAD_DIM = 64
N_HEADS = 8
N_HEAD_PAIRS = N_HEADS // 2
ONES_LANE = (HEAD_DIM, 0)
ISSUE_UNROLL = 8
SGU_CHUNK = 128
SGU_GROUPS = 8
CONV_KERNEL = 31
CONV_HALO = 32
POOL_WINDOWS = (2, 4, 8, 16)
N_BRANCH = 4
N_EXPERTS = 8
WIDTH = 512
NEG = -1e30
LOG2E = math.log2(math.e)
VMEM_LIMIT = 48 * 1024 * 1024


def _cparams(sem, vmem=VMEM_LIMIT):
    return pltpu.CompilerParams(dimension_semantics=sem, vmem_limit_bytes=vmem)


def _rms(x, g):
    return x * lax.rsqrt(jnp.mean(x * x, axis=-1, keepdims=True) + EPS) * g


def _norm_body(x_ref, g_ref, h_ref):
    h_ref[...] = _rms(x_ref[...], g_ref[...]).astype(BF16)


def _norm(x, g, *, tm):
    n, d = x.shape
    row = pl.BlockSpec((tm, d), lambda i: (i, 0))
    return pl.pallas_call(
        _norm_body,
        grid=(n // tm,),
        in_specs=[row, pl.BlockSpec((1, d), lambda i: (0, 0))],
        out_specs=row,
        out_shape=jax.ShapeDtypeStruct((n, d), BF16),
        compiler_params=_cparams(("parallel",)),
        name="norm",
    )(x, g)


def _inproj_body(h_ref, w_ref, wf_ref, o_ref, f_ref):
    @pl.when(pl.program_id(1) == 0)
    def _():
        f_ref[...] = jnp.dot(h_ref[...], wf_ref[...], preferred_element_type=F32)

    o_ref[...] = jnp.dot(h_ref[...], w_ref[...], preferred_element_type=F32).astype(BF16)


def _inproj(h, w, wf, *, tm, tn):
    n, d = h.shape
    ncol = w.shape[1]
    return pl.pallas_call(
        _inproj_body,
        grid=(n // tm, ncol // tn),
        in_specs=[
            pl.BlockSpec((tm, d), lambda i, j: (i, 0)),
            pl.BlockSpec((d, tn), lambda i, j: (0, j)),
            pl.BlockSpec((d, LANES), lambda i, j: (0, 0)),
        ],
        out_specs=[
            pl.BlockSpec((tm, tn), lambda i, j: (i, j)),
            pl.BlockSpec((tm, LANES), lambda i, j: (i, 0)),
        ],
        out_shape=[jax.ShapeDtypeStruct((n, ncol), BF16), jax.ShapeDtypeStruct((n, LANES), F32)],
        compiler_params=_cparams(("parallel", "arbitrary")),
        name="inproj",
    )(h, w, wf)


def _split3(x):
    hi = x.astype(BF16)
    r1 = x - hi.astype(F32)
    mid = r1.astype(BF16)
    lo = (r1 - mid.astype(F32)).astype(BF16)
    return hi, mid, lo


def _cumfg_body(f_ref, b_ref, cc_ref, cr_ref, *, s):
    blk = LANES
    r = lax.broadcasted_iota(jnp.int32, (blk, blk), 0)
    c = lax.broadcasted_iota(jnp.int32, (blk, blk), 1)
    tri = jnp.where(c <= r, 1.0, 0.0).astype(BF16)
    carry = jnp.zeros((1, LANES), F32)
    for j in range(s // blk):
        z = f_ref[j * blk:(j + 1) * blk, :] + b_ref[...]
        logf = jnp.minimum(z, 0.0) - jnp.log1p(jnp.exp(-jnp.abs(z)))
        hi, mid, lo = _split3(logf)
        cs = (jnp.dot(tri, hi, preferred_element_type=F32)
              + jnp.dot(tri, mid, preferred_element_type=F32)
              + jnp.dot(tri, lo, preferred_element_type=F32)) + carry
        cc_ref[j * blk:(j + 1) * blk, :] = cs
        cr_ref[0, :, j * blk:(j + 1) * blk] = cs.T[0:N_HEADS, :]
        carry = cs[blk - 1:blk, :]


def _cumfg(f, b, *, batch, s):
    n = f.shape[0]
    return pl.pallas_call(
        functools.partial(_cumfg_body, s=s),
        grid=(batch,),
        in_specs=[pl.BlockSpec((s, LANES), lambda i: (i, 0)), pl.BlockSpec((1, LANES), lambda i: (0, 0))],
        out_specs=[pl.BlockSpec((s, LANES), lambda i: (i, 0)), pl.BlockSpec((1, N_HEADS, s), lambda i: (i, 0, 0))],
        out_shape=[jax.ShapeDtypeStruct((n, LANES), F32), jax.ShapeDtypeStruct((batch, N_HEADS, s), F32)],
        compiler_params=_cparams(("parallel",)),
        name="cumfg",
    )(f, b)


def _attn_body(q_ref, k_ref, v_ref, cc_ref, cr_ref, gq_ref, gk_ref, o_ref, qn_scr, kn_scr, vn_scr, s_scr, *, tq, s):
    hp = pl.program_id(1)
    lane = lax.broadcasted_iota(jnp.int32, (1, LANES), 1)
    lo = lane < HEAD_DIM

    def headnorm(x, g):
        x2 = x * x
        sa = jnp.sum(jnp.where(lo, x2, 0.0), axis=-1, keepdims=True)
        sb = jnp.sum(jnp.where(lo, 0.0, x2), axis=-1, keepdims=True)
        ms = jnp.where(lo, sa, sb) * (1.0 / HEAD_DIM)
        return x * lax.rsqrt(ms + EPS) * g

    def prep(i, c):
        r = pl.multiple_of(i * tq, tq)
        kn_scr[pl.ds(r, tq), :] = headnorm(k_ref[pl.ds(r, tq), :].astype(F32), gk_ref[...]).astype(BF16)
        q = headnorm(q_ref[pl.ds(r, tq), :].astype(F32), gq_ref[...]) * (HEAD_DIM ** -0.5 * LOG2E)
        qn_scr[0, pl.ds(r, tq), :] = jnp.where(lo, q, 0.0).astype(BF16)
        qn_scr[1, pl.ds(r, tq), :] = jnp.where(lo, 0.0, q).astype(BF16)
        v = v_ref[pl.ds(r, tq), :].astype(F32)
        vn_scr[0, pl.ds(r, tq), :] = jnp.where(lo, v, jnp.where(lane == ONES_LANE[0], 1.0, 0.0)).astype(BF16)
        vn_scr[1, pl.ds(r, tq), :] = jnp.where(lo, jnp.where(lane == ONES_LANE[1], 1.0, 0.0), v).astype(BF16)
        return c
    lax.fori_loop(0, s // tq, prep, 0)

    causal = (lax.broadcasted_iota(jnp.int32, (tq, tq), 1) <= lax.broadcasted_iota(jnp.int32, (tq, tq), 0))

    def scores(qi, a):
        rows = slice(qi * tq, (qi + 1) * tq)
        qh = qn_scr[a, rows, :]
        cq = jnp.sum(jnp.where(lane == 2 * hp + a, cc_ref[rows, :] * LOG2E, 0.0), axis=-1, keepdims=True)
        mrun = jnp.full((tq, LANES), NEG, F32)
        for kb in range(qi + 1):
            kcols = slice(kb * tq, (kb + 1) * tq)
            sc = lax.dot_general(qh, kn_scr[kcols, :], (((1,), (1,)), ((), ())), preferred_element_type=F32)
            sc = sc - cr_ref[0, pl.ds(2 * hp + a, 1), kcols] * LOG2E
            if kb == qi:
                sc = jnp.where(causal, sc, NEG)
            s_scr[a, :, kcols] = sc
            for h in range(tq // LANES):
                mrun = jnp.maximum(mrun, sc[:, h * LANES:(h + 1) * LANES])
        m = jnp.max(mrun, axis=-1, keepdims=True) + cq
        return jnp.broadcast_to(m - cq, (tq, LANES))

    def weighted_values(qi, a, shift):
        acc = jnp.zeros((tq, LANES), F32)
        for kb in range(qi + 1):
            kcols = slice(kb * tq, (kb + 1) * tq)
            p = jnp.concatenate(
                [jnp.exp2(s_scr[a, :, kb * tq + h * LANES:kb * tq + (h + 1) * LANES] - shift).astype(BF16)
                 for h in range(tq // LANES)], axis=1)
            acc = acc + jnp.dot(p, vn_scr[a, kcols, :], preferred_element_type=F32)
        denom = jnp.sum(jnp.where(lane == ONES_LANE[a], acc, 0.0), axis=-1, keepdims=True)
        return acc / denom

    units = [(qi, a) for qi in range(s // tq) for a in range(2)]
    shift = scores(*units[0])
    outs = {}
    for u, (qi, a) in enumerate(units):
        next_shift = scores(*units[u + 1]) if u + 1 < len(units) else None
        outs[a] = weighted_values(qi, a, shift)
        shift = next_shift
        if a == 1:
            o_ref[qi * tq:(qi + 1) * tq, :] = jnp.where(lo, outs[0], outs[1]).astype(BF16)


def _attention(proj, cc, cr, gq2, gk2, *, batch, s, tq):
    n = proj.shape[0]
    return pl.pallas_call(
        functools.partial(_attn_body, tq=tq, s=s),
        grid=(batch, N_HEAD_PAIRS),
        in_specs=[
            pl.BlockSpec((s, LANES), lambda b, h: (b, h)),
            pl.BlockSpec((s, LANES), lambda b, h: (b, N_HEAD_PAIRS + h)),
            pl.BlockSpec((s, LANES), lambda b, h: (b, 2 * N_HEAD_PAIRS + h)),
            pl.BlockSpec((s, LANES), lambda b, h: (b, 0)),
            pl.BlockSpec((1, N_HEADS, s), lambda b, h: (b, 0, 0)),
            pl.BlockSpec((1, LANES), lambda b, h: (0, 0)),
            pl.BlockSpec((1, LANES), lambda b, h: (0, 0)),
        ],
        out_specs=pl.BlockSpec((s, LANES), lambda b, h: (b, h)),
        out_shape=jax.ShapeDtypeStruct((n, WIDTH), BF16),
        scratch_shapes=[pltpu.VMEM((2, s, LANES), BF16), pltpu.VMEM((s, LANES), BF16),
                        pltpu.VMEM((2, s, LANES), BF16), pltpu.VMEM((2, tq, s), F32)],
        compiler_params=_cparams(("parallel", "parallel")),
        name="fox_attention",
    )(proj, proj, proj, cc, cr, gq2, gk2)


def _branch_body(su_ref, sv_ref, cv_ref, cg_ref, p_ref, gs_ref, wsp_ref, bsp_ref, wdw_ref, bdw_ref, gc_ref,
                 wp_ref, ls_ref, ysgu_ref, yconv_ref, ypool_ref, ybuf, pbuf, cacc, *, t):
    ti = pl.program_id(1)
    lane = lax.broadcasted_iota(jnp.int32, (1, LANES), 1)
    lo = lane < (WIDTH // SGU_GROUPS)

    u = jax.nn.gelu(su_ref[...].astype(F32))
    vn = _rms(jax.nn.gelu(sv_ref[...].astype(F32)), gs_ref[...]).astype(BF16)
    r = lax.broadcasted_iota(jnp.int32, (SGU_CHUNK, SGU_CHUNK), 0)
    c = lax.broadcasted_iota(jnp.int32, (SGU_CHUNK, SGU_CHUNK), 1)
    wm = [jnp.where(c <= r, wsp_ref[g], 0.0).astype(BF16) for g in range(SGU_GROUPS)]
    for ch in range(t // SGU_CHUNK):
        rows = slice(ch * SGU_CHUNK, (ch + 1) * SGU_CHUNK)
        for gp in range(SGU_GROUPS // 2):
            cols = slice(gp * LANES, (gp + 1) * LANES)
            v2 = vn[rows, cols]
            m0 = jnp.dot(wm[2 * gp], v2, preferred_element_type=F32)
            m1 = jnp.dot(wm[2 * gp + 1], v2, preferred_element_type=F32)
            mixed = jnp.where(lo, m0, m1) + bsp_ref[:, cols]
            ysgu_ref[rows, cols] = (u[rows, cols] * mixed).astype(BF16)

    @pl.when(ti == 0)
    def _():
        ybuf[0:CONV_HALO, :] = jnp.zeros((CONV_HALO, WIDTH), F32)
        pbuf[0:t, :] = jnp.zeros((t, WIDTH), BF16)
        ybuf[CONV_HALO + t:CONV_HALO + t + SUBLANES, :] = jnp.zeros((SUBLANES, WIDTH), F32)

    ybuf[CONV_HALO:CONV_HALO + t, :] = cv_ref[...].astype(F32) * jax.nn.sigmoid(cg_ref[...].astype(F32))
    rc = 64
    off = CONV_HALO - (CONV_KERNEL - 1)
    for i in range(t // rc):
        for lg in range(WIDTH // LANES):
            cols = slice(lg * LANES, (lg + 1) * LANES)
            acc = jnp.zeros((rc, LANES), F32) + bdw_ref[:, cols]
            for r in range(SUBLANES):
                part = None
                for a in range((off + CONV_KERNEL - 1) // SUBLANES + 1):
                    o = SUBLANES * a + r - off
                    if 0 <= o < CONV_KERNEL:
                        base = i * rc + SUBLANES * a
                        term = ybuf[base:base + rc + SUBLANES, cols] * wdw_ref[o:o + 1, cols]
                        part = term if part is None else part + term
                acc = acc + part[r:r + rc, :]
            cacc[i * rc:(i + 1) * rc, cols] = acc
    nc = 32
    for i in range(t // nc):
        yn = _rms(cacc[i * nc:(i + 1) * nc, :], gc_ref[...])
        yconv_ref[i * nc:(i + 1) * nc, :] = (yn * jax.nn.sigmoid(yn)).astype(BF16)
    ybuf[0:CONV_HALO, :] = ybuf[t:t + CONV_HALO, :]

    p = p_ref[...]
    pbuf[t:2 * t, :] = p
    rr = lax.broadcasted_iota(jnp.int32, (t, 2 * t), 0)
    kk = lax.broadcasted_iota(jnp.int32, (t, 2 * t), 1)
    d = rr + t - kk
    pos1 = ti * t + lax.broadcasted_iota(jnp.int32, (t, 1), 0) + 1
    for gi, w in enumerate(POOL_WINDOWS):
        cols = slice(gi * LANES, (gi + 1) * LANES)
        band = jnp.where((d >= 0) & (d < w), 1.0, 0.0).astype(BF16)
        total = jnp.dot(band, pbuf[:, cols], preferred_element_type=F32)
        cnt = jnp.minimum(pos1, w).astype(F32)
        z = (total / cnt - p[:, cols].astype(F32)).astype(BF16)
        ypool_ref[:, cols] = (jnp.dot(z, wp_ref[gi], preferred_element_type=F32) * ls_ref[:, cols]).astype(BF16)
    pbuf[0:t, :] = p


def _branches(proj, gs, wsp, bsp, wdw, bdw, gc, wp, ls, *, batch, s, t):
    n = proj.shape[0]
    nt = s // t
    seg = lambda k: pl.BlockSpec((t, WIDTH), lambda b, i, k=k: (b * nt + i, k))
    full = lambda shape: pl.BlockSpec(shape, lambda b, i: (0,) * len(shape))
    out = pl.BlockSpec((t, WIDTH), lambda b, i: (b * nt + i, 0))
    return pl.pallas_call(
        functools.partial(_branch_body, t=t),
        grid=(batch, nt),
        in_specs=[seg(3), seg(4), seg(5), seg(6), seg(7),
                  full((1, WIDTH)), full((SGU_GROUPS, SGU_CHUNK, SGU_CHUNK)), full((SGU_CHUNK, WIDTH)),
                  full((CONV_HALO, WIDTH)), full((1, WIDTH)), full((1, WIDTH)),
                  full((len(POOL_WINDOWS), LANES, LANES)), full((1, WIDTH))],
        out_specs=[out, out, out],
        out_shape=[jax.ShapeDtypeStruct((n, WIDTH), BF16)] * 3,
        scratch_shapes=[pltpu.VMEM((CONV_HALO + t + SUBLANES, WIDTH), F32), pltpu.VMEM((2 * t, WIDTH), BF16),
                        pltpu.VMEM((t, WIDTH), F32)],
        compiler_params=_cparams(("parallel", "arbitrary")),
        name="branches",
    )(proj, proj, proj, proj, proj, gs, wsp, bsp, wdw, bdw, gc, wp, ls)


def _merge_body(yf_ref, ys_ref, yc_ref, yp_ref, gl_ref, bg_ref, x_ref, wb_ref, wo_ref, gf_ref, xo_ref, h_ref):
    d = x_ref.shape[1]
    merged = None
    for i, y_ref in enumerate((yf_ref, ys_ref, yc_ref, yp_ref)):
        yb = jnp.dot(y_ref[...], wb_ref[i], preferred_element_type=F32)
        gate = jax.nn.sigmoid(gl_ref[:, i * d:(i + 1) * d].astype(F32) + bg_ref[:, i * d:(i + 1) * d])
        merged = gate * yb if merged is None else merged + gate * yb
    xn = x_ref[...] + jnp.dot(merged.astype(BF16), wo_ref[...], preferred_element_type=F32)
    xo_ref[...] = xn
    h_ref[...] = _rms(xn, gf_ref[...]).astype(h_ref.dtype)


def _merge(yf, ys, yc, yp, proj, bg, x, wb, wo, gf, *, tm, h_dtype):
    n, d = x.shape
    row = lambda w: pl.BlockSpec((tm, w), lambda i: (i, 0))
    return pl.pallas_call(
        _merge_body,
        grid=(n // tm,),
        in_specs=[row(WIDTH), row(WIDTH), row(WIDTH), row(WIDTH),
                  pl.BlockSpec((tm, N_BRANCH * d), lambda i: (i, 1)),
                  pl.BlockSpec((1, N_BRANCH * d), lambda i: (0, 0)),
                  row(d),
                  pl.BlockSpec((N_BRANCH, WIDTH, d), lambda i: (0, 0, 0)),
                  pl.BlockSpec((d, d), lambda i: (0, 0)),
                  pl.BlockSpec((1, d), lambda i: (0, 0))],
        out_specs=[row(d), row(d)],
        out_shape=[jax.ShapeDtypeStruct((n, d), F32), jax.ShapeDtypeStruct((n, d), h_dtype)],
        compiler_params=_cparams(("parallel",)),
        name="merge",
    )(yf, ys, yc, yp, proj, bg, x, wb, wo, gf)


def _dense_ffn_body(h_ref, x_ref, wg_ref, wu_ref, wd_ref, gn_ref, o_ref, hn_ref, a_scr, *, tf):
    h = h_ref[...]
    ff = wg_ref.shape[1]
    for c in range(ff // tf):
        cols = slice(c * tf, (c + 1) * tf)
        g = jnp.dot(h, wg_ref[:, cols], preferred_element_type=F32)
        u = jnp.dot(h, wu_ref[:, cols], preferred_element_type=F32)
        a_scr[:, cols] = (g * jax.nn.sigmoid(g) * u).astype(BF16)
    xo = x_ref[...] + jnp.dot(a_scr[...], wd_ref[...], preferred_element_type=F32)
    o_ref[...] = xo
    hn_ref[...] = _rms(xo, gn_ref[...]).astype(BF16)


def _dense_ffn(h, x, wg, wu, wd, g_next, *, tm, tf):
    n, d = x.shape
    ff = wg.shape[1]
    row = pl.BlockSpec((tm, d), lambda i: (i, 0))
    return pl.pallas_call(
        functools.partial(_dense_ffn_body, tf=tf),
        grid=(n // tm,),
        in_specs=[row, row,
                  pl.BlockSpec((d, ff), lambda i: (0, 0)),
                  pl.BlockSpec((d, ff), lambda i: (0, 0)),
                  pl.BlockSpec((ff, d), lambda i: (0, 0)),
                  pl.BlockSpec((1, d), lambda i: (0, 0))],
        out_specs=[row, row],
        out_shape=[jax.ShapeDtypeStruct((n, d), F32), jax.ShapeDtypeStruct((n, d), BF16)],
        scratch_shapes=[pltpu.VMEM((tm, ff), BF16)],
        compiler_params=_cparams(("parallel",), vmem=56 * 1024 * 1024),
        name="dense_ffn",
    )(h, x, wg, wu, wd, g_next)


def _route_body(h_ref, wr_ref, info_ref, cnt_ref, carry, *, tm):
    i = pl.program_id(0)

    @pl.when(i == 0)
    def _():
        carry[...] = jnp.zeros_like(carry)

    lane = lax.broadcasted_iota(jnp.int32, (tm, LANES), 1)
    logits = jnp.dot(h_ref[...].astype(BF16), wr_ref[...], preferred_element_type=F32)
    logits = jnp.where(lane < N_EXPERTS, logits, NEG)
    lane_f = lane.astype(F32)
    m1 = jnp.max(logits, axis=-1, keepdims=True)
    i1 = jnp.min(jnp.where(logits == m1, lane_f, float(LANES)), axis=-1, keepdims=True)
    rest = jnp.where(lane_f == i1, NEG, logits)
    m2 = jnp.max(rest, axis=-1, keepdims=True)
    i2 = jnp.min(jnp.where(rest == m2, lane_f, float(LANES)), axis=-1, keepdims=True)
    e2 = jnp.exp(m2 - m1)
    g1 = 1.0 / (1.0 + e2)
    g2 = e2 / (1.0 + e2)

    hit1 = lane_f == i1
    hit2 = lane_f == i2
    onehot = jnp.where(hit1 | hit2, 1.0, 0.0).astype(BF16)
    r = lax.broadcasted_iota(jnp.int32, (tm, tm), 0)
    c = lax.broadcasted_iota(jnp.int32, (tm, tm), 1)
    before = jnp.where(c < r, 1.0, 0.0).astype(BF16)
    prior = jnp.dot(before, onehot, preferred_element_type=F32) + carry[0:1, :]
    rank1 = jnp.sum(jnp.where(hit1, prior, 0.0), axis=-1, keepdims=True)
    rank2 = jnp.sum(jnp.where(hit2, prior, 0.0), axis=-1, keepdims=True)
    total = carry[0:1, :] + jnp.sum(onehot.astype(F32), axis=0, keepdims=True)
    carry[0:1, :] = total
    cnt_ref[...] = jnp.broadcast_to(total, cnt_ref.shape)

    info = jnp.where(lane == 0, g1, 0.0)
    info = jnp.where(lane == 1, g2, info)
    info = jnp.where(lane == 2, i1, info)
    info = jnp.where(lane == 3, i2, info)
    info = jnp.where(lane == 4, rank1, info)
    info = jnp.where(lane == 5, rank2, info)
    info_ref[...] = info


def _route(h, wr, *, tm):
    n, d = h.shape
    return pl.pallas_call(
        functools.partial(_route_body, tm=tm),
        grid=(n // tm,),
        in_specs=[pl.BlockSpec((tm, d), lambda i: (i, 0)), pl.BlockSpec((d, LANES), lambda i: (0, 0))],
        out_specs=[pl.BlockSpec((tm, LANES), lambda i: (i, 0)), pl.BlockSpec((8, LANES), lambda i: (0, 0))],
        out_shape=[jax.ShapeDtypeStruct((n, LANES), F32), jax.ShapeDtypeStruct((8, LANES), F32)],
        scratch_shapes=[pltpu.VMEM((8, LANES), F32)],
        compiler_params=_cparams(("arbitrary",)),
        name="route",
    )(h, wr)


def _dispatch_body(d1_ref, d2_ref, h_ref, xs_in_ref, xs_ref, sem, *, td):
    del xs_in_ref
    base = pl.program_id(0) * td

    def start(t, c):
        src = h_ref.at[pl.ds(t, 1), :]
        pltpu.make_async_copy(src, xs_ref.at[pl.ds(d1_ref[base + t], 1), :], sem.at[0]).start()
        pltpu.make_async_copy(src, xs_ref.at[pl.ds(d2_ref[base + t], 1), :], sem.at[1]).start()
        return c

    lax.fori_loop(0, td, start, 0, unroll=ISSUE_UNROLL)
    for k in range(2):
        pltpu.make_async_copy(h_ref, xs_ref.at[pl.ds(0, td), :], sem.at[k]).wait()


def _dispatch(d1, d2, h, xs_init, *, td):
    n, d = h.shape
    return pl.pallas_call(
        functools.partial(_dispatch_body, td=td),
        grid_spec=pltpu.PrefetchScalarGridSpec(
            num_scalar_prefetch=2,
            grid=(n // td,),
            in_specs=[pl.BlockSpec((td, d), lambda i, d1, d2: (i, 0)), pl.BlockSpec(memory_space=pl.ANY)],
            out_specs=pl.BlockSpec(memory_space=pl.ANY),
            scratch_shapes=[pltpu.SemaphoreType.DMA((2,))],
        ),
        out_shape=jax.ShapeDtypeStruct(xs_init.shape, xs_init.dtype),
        input_output_aliases={3: 0},
        compiler_params=_cparams(("arbitrary",)),
        name="dispatch",
    )(d1, d2, h, xs_init)


def _expert_body(be_ref, nu_ref, xs_ref, wg_ref, wu_ref, wd_ref, ys_ref, xb_scr, a_scr, *, tc):
    del be_ref
    b = pl.program_id(0)
    f = pl.program_id(1)
    used = b < nu_ref[0]

    @pl.when(f == 0)
    def _():
        xb_scr[...] = xs_ref[...].astype(BF16)

    @pl.when(jnp.logical_not(used))
    def _():
        ys_ref[...] = jnp.zeros_like(ys_ref)

    @pl.when(used)
    def _():
        xb = xb_scr[...]
        for c in range(wg_ref.shape[1] // tc):
            cols = slice(c * tc, (c + 1) * tc)
            g = jnp.dot(xb, wg_ref[:, cols], preferred_element_type=F32)
            u = jnp.dot(xb, wu_ref[:, cols], preferred_element_type=F32)
            a_scr[:, cols] = (g * jax.nn.sigmoid(g) * u).astype(BF16)
        y = jnp.dot(a_scr[...], wd_ref[...], preferred_element_type=F32)

        @pl.when(f == 0)
        def _():
            ys_ref[...] = y

        @pl.when(f > 0)
        def _():
            ys_ref[...] += y


def _expert_ffn(blk_e, n_used, xs, wg, wu, wd, *, tmb, tf):
    p, d = xs.shape
    ff = wg.shape[2]
    return pl.pallas_call(
        functools.partial(_expert_body, tc=256),
        grid_spec=pltpu.PrefetchScalarGridSpec(
            num_scalar_prefetch=2,
            grid=(p // tmb, ff // tf),
            in_specs=[
                pl.BlockSpec((tmb, d), lambda b, f, be, nu: (b, 0)),
                pl.BlockSpec((None, d, tf), lambda b, f, be, nu: (be[b], 0, f)),
                pl.BlockSpec((None, d, tf), lambda b, f, be, nu: (be[b], 0, f)),
                pl.BlockSpec((None, tf, d), lambda b, f, be, nu: (be[b], f, 0)),
            ],
            out_specs=pl.BlockSpec((tmb, d), lambda b, f, be, nu: (b, 0)),
            scratch_shapes=[pltpu.VMEM((tmb, d), BF16), pltpu.VMEM((tmb, tf), BF16)],
        ),
        out_shape=jax.ShapeDtypeStruct((p, d), F32),
        compiler_params=_cparams(("parallel", "arbitrary")),
        name="expert_ffn",
    )(blk_e, n_used, xs, wg, wu, wd)


def _combine_body(d1_ref, d2_ref, x_ref, info_ref, *refs, tc, emit_h):
    if emit_h:
        gn_ref, ys_ref, o_ref, hn_ref, ya, yb, sem = refs
    else:
        ys_ref, o_ref, ya, yb, sem = refs
    base = pl.program_id(0) * tc

    def start(t, c):
        pltpu.make_async_copy(ys_ref.at[pl.ds(d1_ref[base + t], 1), :], ya.at[pl.ds(t, 1), :], sem.at[0]).start()
        pltpu.make_async_copy(ys_ref.at[pl.ds(d2_ref[base + t], 1), :], yb.at[pl.ds(t, 1), :], sem.at[1]).start()
        return c

    lax.fori_loop(0, tc, start, 0, unroll=ISSUE_UNROLL)
    pltpu.make_async_copy(ys_ref.at[pl.ds(0, tc), :], ya, sem.at[0]).wait()
    pltpu.make_async_copy(ys_ref.at[pl.ds(0, tc), :], yb, sem.at[1]).wait()
    info = info_ref[...]
    xo = x_ref[...] + info[:, 0:1] * ya[...] + info[:, 1:2] * yb[...]
    o_ref[...] = xo
    if emit_h:
        hn_ref[...] = _rms(xo, gn_ref[...]).astype(BF16)


def _combine(d1, d2, x, info, g_next, ys, *, tc):
    n, d = x.shape
    emit_h = g_next is not None
    row = pl.BlockSpec((tc, d), lambda i, d1, d2: (i, 0))
    in_specs = [row, pl.BlockSpec((tc, LANES), lambda i, d1, d2: (i, 0))]
    if emit_h:
        in_specs.append(pl.BlockSpec((1, d), lambda i, d1, d2: (0, 0)))
    in_specs.append(pl.BlockSpec(memory_space=pl.ANY))
    out = pl.pallas_call(
        functools.partial(_combine_body, tc=tc, emit_h=emit_h),
        grid_spec=pltpu.PrefetchScalarGridSpec(
            num_scalar_prefetch=2,
            grid=(n // tc,),
            in_specs=in_specs,
            out_specs=[row, row] if emit_h else [row],
            scratch_shapes=[pltpu.VMEM((tc, d), F32), pltpu.VMEM((tc, d), F32), pltpu.SemaphoreType.DMA((2,))],
        ),
        out_shape=[jax.ShapeDtypeStruct((n, d), F32)] + ([jax.ShapeDtypeStruct((n, d), BF16)] if emit_h else []),
        compiler_params=_cparams(("arbitrary",)),
        name="combine",
    )(d1, d2, x, info, *((g_next,) if emit_h else ()), ys)
    return (out[0], out[1]) if emit_h else (out[0], None)


def _tiles(n, s):
    return dict(
        tm_in=min(1024, n), tn_in=1024,
        tq=min(256, s),
        t_branch=min(256, s),
        tm_merge=min(512, n),
        tm_ffn=min(512, n), tf_ffn=256,
        tm_route=min(512, n),
        t_rows=min(256, n),
        tmb=512, tf_exp=1792,
    )


def _moe_ffn(h, x, wr, wg, wu, wd, g_next, cfg):
    n, d = x.shape
    tmb = cfg["tmb"]
    info, counts = _route(h, wr, tm=cfg["tm_route"])
    sizes = counts[0, :N_EXPERTS].astype(jnp.int32)
    padded = (sizes + tmb - 1) // tmb * tmb
    pend = jnp.cumsum(padded)
    pstart = pend - padded
    e1 = info[:, 2].astype(jnp.int32)
    e2 = info[:, 3].astype(jnp.int32)
    d1 = pstart[e1] + info[:, 4].astype(jnp.int32)
    d2 = pstart[e2] + info[:, 5].astype(jnp.int32)
    p = (2 * n + tmb - 1) // tmb * tmb + N_EXPERTS * tmb
    nb = p // tmb
    blk_e = jnp.minimum(jnp.searchsorted(pend, jnp.arange(nb, dtype=jnp.int32) * tmb, side="right"),
                        N_EXPERTS - 1).astype(jnp.int32)
    n_used = (pend[-1:] // tmb).astype(jnp.int32)
    blk_e = jnp.where(jnp.arange(nb) < n_used[0], blk_e, blk_e[jnp.maximum(n_used[0] - 1, 0)])
    xs = _dispatch(d1, d2, h, jnp.zeros((p, d), F32), td=cfg["t_rows"])
    ys = _expert_ffn(blk_e, n_used, xs, wg, wu, wd, tmb=tmb, tf=cfg["tf_exp"])
    return _combine(d1, d2, x, info, g_next, ys, tc=cfg["t_rows"])


def kernel(x, g_mix, w_in, b_forget, g_q, g_k, g_sgu, w_spatial, b_spatial, w_dwconv, b_dwconv, g_conv, w_pool,
           pool_scale, w_branch, b_gate, w_out, g_ffn, w_ffn_gate, w_ffn_up, w_ffn_down, w_router, w_exp_gate,
           w_exp_up, w_exp_down):
    batch, s, d = x.shape
    n = batch * s
    depth = w_in.shape[0]
    assert depth % 2 == 0, "layers alternate dense / expert FFN and the trunk ends on an expert layer"
    cfg = _tiles(n, s)
    fox_w = N_HEADS * HEAD_DIM
    f_lo, f_hi = 3 * fox_w, 3 * fox_w + N_HEADS

    xf = x.reshape(n, d)
    h = _norm(xf, g_mix[0][None], tm=cfg["tm_merge"])
    for l in range(depth):
        w_main = jnp.concatenate([w_in[l][:, :f_lo], w_in[l][:, f_hi:]], axis=1).astype(BF16)
        w_f = jnp.pad(w_in[l][:, f_lo:f_hi], ((0, 0), (0, LANES - N_HEADS))).astype(BF16)
        proj, f_logit = _inproj(h, w_main, w_f, tm=cfg["tm_in"], tn=cfg["tn_in"])

        b_f = jnp.pad(b_forget[l], (0, LANES - N_HEADS))[None]
        cc, cr = _cumfg(f_logit, b_f, batch=batch, s=s)
        y_fox = _attention(proj, cc, cr, jnp.tile(g_q[l], 2)[None], jnp.tile(g_k[l], 2)[None],
                           batch=batch, s=s, tq=cfg["tq"])

        b_sp = jnp.repeat(b_spatial[l].T, WIDTH // SGU_GROUPS, axis=1)
        w_dw = jnp.pad(w_dwconv[l], ((0, CONV_HALO - CONV_KERNEL), (0, 0)))
        y_sgu, y_conv, y_pool = _branches(
            proj, g_sgu[l][None], w_spatial[l], b_sp, w_dw, b_dwconv[l][None], g_conv[l][None],
            w_pool[l].astype(BF16), pool_scale[l][None], batch=batch, s=s, t=cfg["t_branch"])

        moe = l % 2 == 1
        xf, h2 = _merge(y_fox, y_sgu, y_conv, y_pool, proj, b_gate[l].reshape(1, N_BRANCH * d), xf,
                        w_branch[l].astype(BF16), w_out[l].astype(BF16), g_ffn[l][None],
                        tm=cfg["tm_merge"], h_dtype=F32 if moe else BF16)
        i = l // 2
        g_next = g_mix[l + 1][None] if l + 1 < depth else None
        if moe:
            w_r = jnp.pad(w_router[i], ((0, 0), (0, LANES - N_EXPERTS))).astype(BF16)
            xf, h = _moe_ffn(h2, xf, w_r, w_exp_gate[i].astype(BF16), w_exp_up[i].astype(BF16),
                             w_exp_down[i].astype(BF16), g_next, cfg)
        else:
            xf, h = _dense_ffn(h2, xf, w_ffn_gate[i].astype(BF16), w_ffn_up[i].astype(BF16),
                               w_ffn_down[i].astype(BF16), g_next, tm=cfg["tm_ffn"], tf=cfg["tf_ffn"])
    return xf.reshape(batch, s, d)
```
